```python
import jax, jax.numpy as jnp
from jax import lax
import numpy as np

D_MODEL = 4096
BATCH = 2
SEQ = 4096
DEPTH = 2

HEAD_DIM = 128
N_MIX_HEADS = D_MODEL // HEAD_DIM
DN_HEADS = 3 * N_MIX_HEADS // 8
RET_HEADS = N_MIX_HEADS // 4
FOX_HEADS = N_MIX_HEADS - DN_HEADS - RET_HEADS
DN_W = DN_HEADS * HEAD_DIM
RET_W = RET_HEADS * HEAD_DIM
FOX_W = FOX_HEADS * HEAD_DIM
D_MIX = DN_W + RET_W + FOX_W
IN_WIDTHS = (3 * DN_W, DN_W, DN_HEADS, DN_HEADS,
             RET_W, RET_W, RET_W, RET_W,
             FOX_W, FOX_W, FOX_W, FOX_W, FOX_HEADS)
N_IN = sum(IN_WIDTHS)
CONV_K = 4
DN_CHUNK = 64
RET_CHUNK = 128
FOX_BLOCK = 128
ROPE_BASE = 10000.0
N_EXPERTS = 32
TOP_K = 4
D_EXPERT = 512
MOE_BLOCK = 128
SWIGLU_LIMIT = 7.0
SWIGLU_ALPHA = 1.702
EPS = 1e-6

kernel_name = "hybrid_deltanet_retention_fox_moe_adaln"


def _rms(x):
    xf = x.astype(jnp.float32)
    return xf * lax.rsqrt(jnp.mean(xf * xf, axis=-1, keepdims=True) + EPS)


def rms_norm(x, g):
    return (_rms(x) * g.astype(jnp.float32)).astype(x.dtype)


def split_heads(t, n_heads):
    b, s, _ = t.shape
    return t.reshape(b, s, n_heads, -1).transpose(0, 2, 1, 3)


def merge_heads(t):
    b, h, s, d = t.shape
    return t.transpose(0, 2, 1, 3).reshape(b, s, h * d)


def l2norm(t):
    return t * lax.rsqrt(jnp.sum(t * t, axis=-1, keepdims=True) + EPS)


def causal_depthwise_conv(u, w):
    s = u.shape[1]
    up = jnp.pad(u, ((0, 0), (CONV_K - 1, 0), (0, 0)))
    return sum(up[:, j:j + s] * w[j] for j in range(CONV_K))


def rope(t):
    s, d = t.shape[-2], t.shape[-1]
    half = d // 2
    inv = ROPE_BASE ** (-jnp.arange(half, dtype=jnp.float32) / half)
    ang = jnp.arange(s, dtype=jnp.float32)[:, None] * inv[None, :]
    cos, sin = jnp.cos(ang), jnp.sin(ang)
    t1, t2 = t[..., :half], t[..., half:]
    return jnp.concatenate([t1 * cos - t2 * sin, t1 * sin + t2 * cos], axis=-1)


def gated_delta_rule(q, k, v, beta, g):
    b, h, s, dk = q.shape
    dv = v.shape[-1]
    c = DN_CHUNK
    n = s // c
    q = q.reshape(b, h, n, c, dk)
    k = k.reshape(b, h, n, c, dk)
    v = v.reshape(b, h, n, c, dv)
    beta = beta.reshape(b, h, n, c)
    gc = jnp.cumsum(g.reshape(b, h, n, c), axis=-1)
    idx = jnp.arange(c)
    incl = idx[:, None] >= idx[None, :]
    strict = idx[:, None] > idx[None, :]
    gamma = jnp.exp(jnp.where(incl, gc[..., :, None] - gc[..., None, :], -jnp.inf))
    kk = jnp.einsum('bhncd,bhnmd->bhncm', k, k)
    a_mat = jnp.where(strict, beta[..., :, None] * kk * gamma, 0.0)
    lhs = a_mat + jnp.eye(c, dtype=q.dtype)
    rhs = jnp.concatenate([k * (beta * jnp.exp(gc))[..., None], v * beta[..., None]], axis=-1)
    sol = lax.linalg.triangular_solve(lhs, rhs, left_side=True, lower=True, unit_diagonal=True)
    w_c, u_c = sol[..., :dk], sol[..., dk:]
    attn = jnp.einsum('bhncd,bhnmd->bhncm', q, k) * gamma
    q_dec = q * jnp.exp(gc)[..., None]
    k_dec = k * jnp.exp(gc[..., -1:] - gc)[..., None]
    g_last = jnp.exp(gc[..., -1])
    xs = tuple(jnp.moveaxis(t, 2, 0) for t in (w_c, u_c, attn, q_dec, k_dec, g_last))

    def step(state, inp):
        w_i, u_i, attn_i, qd_i, kd_i, gl_i = inp
        v_new = u_i - jnp.einsum('bhcd,bhde->bhce', w_i, state)
        o_i = (jnp.einsum('bhcd,bhde->bhce', qd_i, state)
               + jnp.einsum('bhcm,bhme->bhce', attn_i, v_new))
        state = state * gl_i[..., None, None] + jnp.einsum('bhcd,bhce->bhde', kd_i, v_new)
        return state, o_i

    s0 = jnp.zeros((b, h, dk, dv), q.dtype)
    _, o = lax.scan(step, s0, xs)
    return jnp.moveaxis(o, 0, 2).reshape(b, h, s, dv)


def retention(q, k, v):
    b, h, s, d = q.shape
    c = RET_CHUNK
    n = s // c
    lg = jnp.log(1.0 - 2.0 ** (-5.0 - jnp.arange(h, dtype=jnp.float32)))
    lgc = lg[:, None]
    q = q.reshape(b, h, n, c, d)
    k = k.reshape(b, h, n, c, d)
    v = v.reshape(b, h, n, c, d)
    pos = jnp.arange(c, dtype=jnp.float32)
    diff = pos[:, None] - pos[None, :]
    decay = jnp.where(diff >= 0, jnp.exp(lg[:, None, None] * jnp.maximum(diff, 0.0)), 0.0)
    inner = jnp.einsum('bhncd,bhnmd->bhncm', q, k) * decay[:, None]
    q_dec = q * jnp.exp(lgc * (pos + 1.0))[:, None, :, None]
    k_dec = k * jnp.exp(lgc * (c - 1.0 - pos))[:, None, :, None]
    kv = jnp.einsum('bhncd,bhnce->bhnde', k_dec, v)
    chunk_decay = jnp.exp(lg * c)[:, None, None]

    def step(r, kv_i):
        return r * chunk_decay + kv_i, r

    _, r_prev = lax.scan(step, jnp.zeros((b, h, d, d), q.dtype), jnp.moveaxis(kv, 2, 0))
    r_prev = jnp.moveaxis(r_prev, 0, 2)
    o = (jnp.einsum('bhncm,bhnme->bhnce', inner, v)
         + jnp.einsum('bhncd,bhnde->bhnce', q_dec, r_prev))
    return o.reshape(b, h, s, d)


def forgetting_attention(q, k, v, log_f):
    b, h, s, d = q.shape
    nb = s // FOX_BLOCK
    scale = d ** -0.5
    f_cum = jnp.cumsum(log_f, axis=-1)
    kpos = jnp.arange(s)
    qb = jnp.moveaxis(q.reshape(b, h, nb, FOX_BLOCK, d), 2, 0)
    fb = jnp.moveaxis(f_cum.reshape(b, h, nb, FOX_BLOCK), 2, 0)
    pb = kpos.reshape(nb, FOX_BLOCK)

    def block(args):
        q_i, f_i, p_i = args
        logits = (jnp.einsum('bhqd,bhkd->bhqk', q_i, k) * scale
                  + f_i[..., :, None] - f_cum[..., None, :])
        logits = jnp.where(p_i[:, None] >= kpos[None, :], logits, -jnp.inf)
        p = jax.nn.softmax(logits, axis=-1)
        return jnp.einsum('bhqk,bhkd->bhqd', p, v)

    o = lax.map(block, (qb, fb, pb))
    return jnp.moveaxis(o, 0, 2).reshape(b, h, s, d)


def hybrid_mixer(h, w_in, conv_w, a_log, dt_bias, dn_norm_g, f_bias, w_out):
    f32 = jnp.float32
    proj = (h @ w_in).astype(f32)
    cuts = np.cumsum(IN_WIDTHS)[:-1].tolist()
    (dn_qkv, dn_z, dn_b, dn_a, r_q, r_k, r_v, r_g,
     f_q, f_k, f_v, f_z, f_f) = jnp.split(proj, cuts, axis=-1)
    u = jax.nn.silu(causal_depthwise_conv(dn_qkv, conv_w.astype(f32)))
    d_q, d_k, d_v = jnp.split(u, 3, axis=-1)
    d_q = l2norm(split_heads(d_q, DN_HEADS)) * HEAD_DIM ** -0.5
    d_k = l2norm(split_heads(d_k, DN_HEADS))
    d_v = split_heads(d_v, DN_HEADS)
    beta = jax.nn.sigmoid(dn_b).transpose(0, 2, 1)
    g = (-jnp.exp(a_log.astype(f32)) * jax.nn.softplus(dn_a + dt_bias.astype(f32))).transpose(0, 2, 1)
    o_dn = gated_delta_rule(d_q, d_k, d_v, beta, g)
    o_dn = _rms(o_dn) * dn_norm_g.astype(f32) * jax.nn.silu(split_heads(dn_z, DN_HEADS))
    rq = rope(split_heads(r_q, RET_HEADS))
    rk = rope(split_heads(r_k, RET_HEADS)) * HEAD_DIM ** -0.5
    o_ret = retention(rq, rk, split_heads(r_v, RET_HEADS))
    o_ret = _rms(o_ret) * jax.nn.silu(split_heads(r_g, RET_HEADS))
    log_f = jax.nn.log_sigmoid(f_f + f_bias.astype(f32)).transpose(0, 2, 1)
    o_fox = forgetting_attention(split_heads(f_q, FOX_HEADS), split_heads(f_k, FOX_HEADS),
                                 split_heads(f_v, FOX_HEADS), log_f)
    o_fox = o_fox * jax.nn.sigmoid(split_heads(f_z, FOX_HEADS))
    o = jnp.concatenate([merge_heads(o_dn), merge_heads(o_ret), merge_heads(o_fox)], axis=-1)
    return o.astype(h.dtype) @ w_out


def moe_ffn(h, w_router, b_router, w_gu, b_gu, w_down, b_down):
    bsz, s, d = h.shape
    t = bsz * s
    hf = h.reshape(t, d)
    logits = (hf @ w_router + b_router).astype(jnp.float32)
    top_v, top_e = lax.top_k(logits, TOP_K)
    top_w = jax.nn.softmax(top_v, axis=-1)
    n_a = t * TOP_K
    e_a = top_e.reshape(n_a)
    t_a = jnp.repeat(jnp.arange(t, dtype=jnp.int32), TOP_K)
    w_a = top_w.reshape(n_a)
    order = jnp.argsort(e_a)
    e_s, t_s, w_s = e_a[order], t_a[order], w_a[order]
    counts = jnp.bincount(e_a, length=N_EXPERTS)
    start = jnp.cumsum(counts) - counts
    padded = (counts + MOE_BLOCK - 1) // MOE_BLOCK * MOE_BLOCK
    pend = jnp.cumsum(padded)
    pstart = pend - padded
    dest = pstart[e_s] + jnp.arange(n_a) - start[e_s]
    n_rows = n_a + N_EXPERTS * MOE_BLOCK
    n_blk = n_rows // MOE_BLOCK
    row_tok = jnp.full((n_rows,), t, jnp.int32).at[dest].set(t_s)
    row_w = jnp.zeros((n_rows,), jnp.float32).at[dest].set(w_s)
    blk_e = jnp.minimum(jnp.searchsorted(pend, jnp.arange(n_blk) * MOE_BLOCK, side='right'),
                        N_EXPERTS - 1)
    x_pad = jnp.concatenate([hf, jnp.zeros((1, d), hf.dtype)], axis=0)

    def expert_block(args):
        tok, e = args
        xb = x_pad[tok]
        gu = xb @ w_gu[e] + b_gu[e]
        gate = jnp.minimum(gu[:, 0::2], SWIGLU_LIMIT)
        lin = jnp.clip(gu[:, 1::2], -SWIGLU_LIMIT, SWIGLU_LIMIT)
        act = gate * jax.nn.sigmoid(SWIGLU_ALPHA * gate) * (lin + 1.0)
        return act @ w_down[e] + b_down[e]

    y = lax.map(expert_block, (row_tok.reshape(n_blk, MOE_BLOCK), blk_e))
    y = y.reshape(n_rows, d) * row_w[:, None].astype(y.dtype)
    out = jnp.zeros((t + 1, d), y.dtype).at[row_tok].add(y)[:t]
    return out.reshape(bsz, s, d)


def setup_inputs(seed: int = 0) -> dict:
    key = jax.random.key(seed)
    ks = jax.random.split(key, 24)
    f32 = jnp.float32

    def nrm(k, shape, sd):
        return sd * jax.random.normal(k, shape, f32)

    dt = jnp.exp(jax.random.uniform(ks[9], (DEPTH, DN_HEADS), f32, np.log(1e-3), np.log(1e-1)))
    return {
        "x": nrm(ks[0], (BATCH, SEQ, D_MODEL), 1.0),
        "c": nrm(ks[1], (BATCH, D_MODEL), 1.0),
        "norm1_g": 1.0 + nrm(ks[2], (DEPTH, D_MODEL), 0.02),
        "w_ada": nrm(ks[3], (DEPTH, D_MODEL, 6 * D_MODEL), 0.5 * D_MODEL ** -0.5),
        "b_ada": nrm(ks[4], (DEPTH, 6 * D_MODEL), 0.02),
        "w_in": nrm(ks[5], (DEPTH, D_MODEL, N_IN), D_MODEL ** -0.5),
        "dn_conv_w": nrm(ks[6], (DEPTH, CONV_K, 3 * DN_W), CONV_K ** -0.5),
        "dn_a_log": jnp.log(jax.random.uniform(ks[7], (DEPTH, DN_HEADS), f32, 1.0, 16.0)),
        "dn_dt_bias": dt + jnp.log(-jnp.expm1(-dt)),
        "dn_norm_g": 1.0 + nrm(ks[8], (DEPTH, HEAD_DIM), 0.02),
        "fox_f_bias": 2.0 + nrm(ks[10], (DEPTH, FOX_HEADS), 0.5),
        "w_out": nrm(ks[11], (DEPTH, D_MIX, D_MODEL), D_MIX ** -0.5),
        "norm2_g": 1.0 + nrm(ks[12], (DEPTH, D_MODEL), 0.02),
        "w_router": nrm(ks[13], (DEPTH, D_MODEL, N_EXPERTS), D_MODEL ** -0.5),
        "b_router": nrm(ks[14], (DEPTH, N_EXPERTS), 0.01),
        "w_gu": nrm(ks[15], (DEPTH, N_EXPERTS, D_MODEL, 2 * D_EXPERT), D_MODEL ** -0.5),
        "b_gu": nrm(ks[16], (DEPTH, N_EXPERTS, 2 * D_EXPERT), 0.01),
        "w_down": nrm(ks[17], (DEPTH, N_EXPERTS, D_EXPERT, D_MODEL), D_EXPERT ** -0.5),
        "b_down": nrm(ks[18], (DEPTH, N_EXPERTS, D_MODEL), 0.01),
        "final_norm_g": 1.0 + nrm(ks[19], (D_MODEL,), 0.02),
    }


def reference(x, c, norm1_g, w_ada, b_ada, w_in, dn_conv_w, dn_a_log, dn_dt_bias, dn_norm_g,
              fox_f_bias, w_out, norm2_g, w_router, b_router, w_gu, b_gu, w_down, b_down,
              final_norm_g):
    for l in range(DEPTH):
        mod = c @ w_ada[l] + b_ada[l]
        sh1, sc1, g1, sh2, sc2, g2 = jnp.split(mod, 6, axis=-1)
        h = rms_norm(x, norm1_g[l]) * (1.0 + sc1[:, None, :]) + sh1[:, None, :]
        x = x + g1[:, None, :] * hybrid_mixer(h, w_in[l], dn_conv_w[l], dn_a_log[l], dn_dt_bias[l],
                                              dn_norm_g[l], fox_f_bias[l], w_out[l])
        h = rms_norm(x, norm2_g[l]) * (1.0 + sc2[:, None, :]) + sh2[:, None, :]
        x = x + g2[:, None, :] * moe_ffn(h, w_router[l], b_router[l], w_gu[l], b_gu[l],
                                         w_down[l], b_down[l])
    return rms_norm(x, final_norm_g)
```

```python
import functools

import numpy as np
import jax
import jax.numpy as jnp
from jax import lax
from jax.experimental import pallas as pl
from jax.experimental.pallas import tpu as pltpu

F32 = jnp.float32
BF16 = jnp.bfloat16

HEAD_DIM = 128
CONV_K = 4
DN_CHUNK = 64
RET_CHUNK = 128
ROPE_BASE = 10000.0
TOP_K = 4
SWIGLU_LIMIT = 7.0
SWIGLU_ALPHA = 1.702
EPS = 1e-6
DN_GROUP = 4
LANES = 128
VMEM_LIMIT = 56 * 1024 * 1024


def _cparams(n_axes):
    return pltpu.CompilerParams(dimension_semantics=("arbitrary",) * n_axes,
                                vmem_limit_bytes=VMEM_LIMIT)


def _dot(a, b):
    return jnp.dot(a, b, preferred_element_type=F32)


def _dot_nt(a, b):
    return lax.dot_general(a, b, (((1,), (1,)), ((), ())), preferred_element_type=F32)


def _dot_tn(a, b):
    return lax.dot_general(a, b, (((0,), (0,)), ((), ())), preferred_element_type=F32)


def _dot_hi(a, b):
    return jnp.dot(a, b, preferred_element_type=F32, precision=lax.Precision.HIGHEST)


def _split3(x):
    h = x.astype(BF16)
    r = x - h.astype(F32)
    m = r.astype(BF16)
    l = (r - m.astype(F32)).astype(BF16)
    return h, m, l


def _sigmoid(x):
    return 1.0 / (1.0 + jnp.exp(-x))


def _silu(x):
    return x * _sigmoid(x)


def _softplus(x):
    return jnp.maximum(x, 0.0) + jnp.log1p(jnp.exp(-jnp.abs(x)))


def _log_sigmoid(x):
    return jnp.minimum(x, 0.0) - jnp.log1p(jnp.exp(-jnp.abs(x)))


def _rms_rows(x):
    return x * lax.rsqrt(jnp.mean(x * x, axis=-1, keepdims=True) + EPS)


def _ada_kernel(c_ref, w_ref, b_ref, o_ref):
    o_ref[...] = _dot(c_ref[...], w_ref[...].astype(BF16)) + b_ref[...]


def ada_mod(c_pad, w_ada, b_ada, tn=512):
    depth, d, n = w_ada.shape
    return pl.pallas_call(
        _ada_kernel,
        out_shape=jax.ShapeDtypeStruct((depth, 8, n), F32),
        grid=(depth, n // tn),
        in_specs=[pl.BlockSpec((8, d), lambda l, j: (0, 0)),
                  pl.BlockSpec((None, d, tn), lambda l, j: (l, 0, j)),
                  pl.BlockSpec((None, 1, tn), lambda l, j: (l, 0, j))],
        out_specs=pl.BlockSpec((None, 8, tn), lambda l, j: (l, 0, j)),
        compiler_params=_cparams(2), name="ada_mod",
    )(c_pad, w_ada, b_ada.reshape(depth, 1, n))


def _norm_mod_kernel(x_ref, g_ref, sc_ref, sh_ref, h_ref):
    x = x_ref[...]
    h = _rms_rows(x) * g_ref[...] * (1.0 + sc_ref[0]) + sh_ref[0]
    h_ref[...] = h.astype(h_ref.dtype)


def norm_mod(x2d, g, sc, sh, seq, tm=256):
    t, d = x2d.shape
    tm = min(tm, seq)
    nb = seq // tm
    return pl.pallas_call(
        _norm_mod_kernel,
        out_shape=jax.ShapeDtypeStruct((t, d), BF16),
        grid=(t // tm,),
        in_specs=[pl.BlockSpec((tm, d), lambda i: (i, 0)),
                  pl.BlockSpec((1, d), lambda i: (0, 0)),
                  pl.BlockSpec((1, 1, d), lambda i: (i // nb, 0, 0)),
                  pl.BlockSpec((1, 1, d), lambda i: (i // nb, 0, 0))],
        out_specs=pl.BlockSpec((tm, d), lambda i: (i, 0)),
        compiler_params=_cparams(1), name="norm_mod",
    )(x2d, g.reshape(1, d), sc, sh)


def _mm_kernel(x_ref, w_ref, o_ref):
    o_ref[...] = _dot(x_ref[...], w_ref[...]).astype(o_ref.dtype)


def matmul_bf16(x, w, out_dtype, tm=1024, tn=512):
    m, k = x.shape
    n = w.shape[1]
    tn = min(tn, n)
    return pl.pallas_call(
        _mm_kernel,
        out_shape=jax.ShapeDtypeStruct((m, n), out_dtype),
        grid=(n // tn, m // tm),
        in_specs=[pl.BlockSpec((tm, k), lambda j, i: (i, 0)),
                  pl.BlockSpec((k, tn), lambda j, i: (0, j))],
        out_specs=pl.BlockSpec((tm, tn), lambda j, i: (i, j)),
        compiler_params=_cparams(2), name="in_proj",
    )(x, w)


def _mm_resid_kernel(a_ref, w_ref, x_ref, g_ref, o_ref, wb_ref):
    @pl.when(pl.program_id(1) == 0)
    def _():
        wb_ref[...] = w_ref[...].astype(BF16)
    o_ref[...] = x_ref[...] + g_ref[0] * _dot(a_ref[...], wb_ref[...])


def out_proj_resid(a, w, x2d, gate, seq, tm=1024, tn=512):
    m, k = a.shape
    n = w.shape[1]
    tm = min(tm, seq)
    nb = seq // tm
    return pl.pallas_call(
        _mm_resid_kernel,
        out_shape=jax.ShapeDtypeStruct((m, n), F32),
        grid=(n // tn, m // tm),
        in_specs=[pl.BlockSpec((tm, k), lambda j, i: (i, 0)),
                  pl.BlockSpec((k, tn), lambda j, i: (0, j)),
                  pl.BlockSpec((tm, tn), lambda j, i: (i, j)),
                  pl.BlockSpec((1, 1, tn), lambda j, i: (i // nb, 0, j))],
        out_specs=pl.BlockSpec((tm, tn), lambda j, i: (i, j)),
        scratch_shapes=[pltpu.VMEM((k, tn), BF16)],
        compiler_params=_cparams(2), name="out_proj",
    )(a, w, x2d, gate)


def _fprep_kernel(s_ref, b_ref, f_ref, carry_ref):
    @pl.when(pl.program_id(1) == 0)
    def _():
        carry_ref[...] = jnp.zeros_like(carry_ref)
    tq = s_ref.shape[0]
    lf = _log_sigmoid(s_ref[...] + b_ref[...])
    row = lax.broadcasted_iota(jnp.int32, (tq, tq), 0)
    col = lax.broadcasted_iota(jnp.int32, (tq, tq), 1)
    tri = jnp.where(row >= col, 1.0, 0.0).astype(BF16)
    h, m, l = _split3(lf)
    f = _dot(tri, h) + _dot(tri, m) + _dot(tri, l) + carry_ref[0:1, :]
    f_ref[...] = f
    carry_ref[...] = jnp.broadcast_to(f[tq - 1:tq, :], carry_ref.shape)


def fox_prefix(small, bias_row, batch, seq, col_block, tq=512):
    t = small.shape[0]
    nq = seq // tq
    return pl.pallas_call(
        _fprep_kernel,
        out_shape=jax.ShapeDtypeStruct((t, LANES), F32),
        grid=(batch, nq),
        in_specs=[pl.BlockSpec((tq, LANES), lambda b, i: (b * nq + i, col_block)),
                  pl.BlockSpec((1, LANES), lambda b, i: (0, 0))],
        out_specs=pl.BlockSpec((tq, LANES), lambda b, i: (b * nq + i, 0)),
        scratch_shapes=[pltpu.VMEM((8, LANES), F32)],
        compiler_params=_cparams(2), name="fox_prefix",
    )(small, bias_row)


def _fox_kernel(q_ref, k_ref, v_ref, z_ref, f_ref, ft_ref, o_ref, m_ref, l_ref, acc_ref, *, tq):
    h = pl.program_id(1)
    qi = pl.program_id(2)
    scale = HEAD_DIM ** -0.5
    q = (q_ref[...].astype(F32) * scale).astype(BF16)
    lane = lax.broadcasted_iota(jnp.int32, (tq, LANES), 1)
    fq = jnp.sum(jnp.where(lane == h, f_ref[...], 0.0), axis=-1, keepdims=True)

    m_ref[...] = jnp.full_like(m_ref, -1e30)
    l_ref[...] = jnp.zeros_like(l_ref)
    acc_ref[...] = jnp.zeros_like(acc_ref)

    def block(kj, masked):
        k0 = pl.multiple_of(kj * tq, tq)
        k = k_ref[pl.ds(k0, tq), :]
        v = v_ref[pl.ds(k0, tq), :]
        fk = ft_ref[0, pl.ds(h, 1), pl.ds(k0, tq)]
        s = _dot_nt(q, k) + (fq - fk)
        if masked:
            row = lax.broadcasted_iota(jnp.int32, (tq, tq), 0)
            col = lax.broadcasted_iota(jnp.int32, (tq, tq), 1)
            s = jnp.where(row >= col, s, -1e30)
        m_old = m_ref[...]
        m_new = jnp.maximum(m_old, jnp.max(s, axis=-1, keepdims=True))
        alpha = jnp.exp(m_old - m_new)
        p = jnp.exp(s - m_new)
        l_ref[...] = alpha * l_ref[...] + jnp.sum(p, axis=-1, keepdims=True)
        acc_ref[...] = alpha * acc_ref[...] + _dot(p.astype(BF16), v)
        m_ref[...] = m_new

    def body(kj, carry):
        block(kj, False)
        return carry

    lax.fori_loop(0, qi, body, 0)
    block(qi, True)
    o = acc_ref[...] / l_ref[...]
    o_ref[...] = (o * _sigmoid(z_ref[...].astype(F32))).astype(o_ref.dtype)


def fox_attention(p, f_cum, f_cum_t, batch, seq, n_heads, q_blk, k_blk, v_blk, z_blk, tq=512):
    t = p.shape[0]
    nq = seq // tq
    kern = functools.partial(_fox_kernel, tq=tq)
    return pl.pallas_call(
        kern,
        out_shape=jax.ShapeDtypeStruct((t, n_heads * HEAD_DIM), BF16),
        grid=(batch, n_heads, nq),
        in_specs=[pl.BlockSpec((tq, HEAD_DIM), lambda b, h, i: (b * nq + i, q_blk + h)),
                  pl.BlockSpec((seq, HEAD_DIM), lambda b, h, i: (b, k_blk + h)),
                  pl.BlockSpec((seq, HEAD_DIM), lambda b, h, i: (b, v_blk + h)),
                  pl.BlockSpec((tq, HEAD_DIM), lambda b, h, i: (b * nq + i, z_blk + h)),
                  pl.BlockSpec((tq, LANES), lambda b, h, i: (b * nq + i, 0)),
                  pl.BlockSpec((1, f_cum_t.shape[1], seq), lambda b, h, i: (b, 0, 0))],
        out_specs=pl.BlockSpec((tq, HEAD_DIM), lambda b, h, i: (b * nq + i, h)),
        scratch_shapes=[pltpu.VMEM((tq, 1), F32), pltpu.VMEM((tq, 1), F32),
                        pltpu.VMEM((tq, HEAD_DIM), F32)],
        compiler_params=_cparams(3), name="fox_attn",
    )(p, p, p, p, f_cum, f_cum_t)


def _ret_kernel(q_ref, k_ref, v_ref, g_ref, cos_ref, sin_ref, lg_ref, o_ref, r_ref, *, tc):
    c = RET_CHUNK

    @pl.when(pl.program_id(2) == 0)
    def _():
        r_ref[...] = jnp.zeros_like(r_ref)

    lg = lg_ref[0:1, 0:1]
    rowi = lax.broadcasted_iota(jnp.int32, (c, c), 0)
    coli = lax.broadcasted_iota(jnp.int32, (c, c), 1)
    diff = (rowi - coli).astype(F32)
    decay = jnp.where(diff >= 0, jnp.exp(lg * jnp.maximum(diff, 0.0)), 0.0)
    pos = lax.broadcasted_iota(jnp.int32, (c, 1), 0).astype(F32)
    q_scale = jnp.exp(lg * (pos + 1.0))
    k_scale = jnp.exp(lg * (c - 1.0 - pos))
    chunk_decay = jnp.exp(lg * c)

    def rope(t, cos2, sin2):
        return t * cos2 + pltpu.roll(t, HEAD_DIM // 2, 1) * sin2

    for ci in range(tc // c):
        sl = slice(ci * c, (ci + 1) * c)
        cos2 = cos_ref[sl, :]
        sin2 = sin_ref[sl, :]
        q = rope(q_ref[sl, :].astype(F32), cos2, sin2)
        k = rope(k_ref[sl, :].astype(F32), cos2, sin2) * (HEAD_DIM ** -0.5)
        v = v_ref[sl, :]
        inner = _dot_nt(q.astype(BF16), k.astype(BF16)) * decay
        r_prev = r_ref[...]
        o = _dot(inner.astype(BF16), v) + _dot((q * q_scale).astype(BF16), r_prev.astype(BF16))
        r_ref[...] = r_prev * chunk_decay + _dot_tn((k * k_scale).astype(BF16), v)
        o = _rms_rows(o) * _silu(g_ref[sl, :].astype(F32))
        o_ref[sl, :] = o.astype(o_ref.dtype)


def retention_mix(p, cos2, sin2, lg_tab, batch, seq, n_heads, q_blk, k_blk, v_blk, g_blk, tc=512):
    t = p.shape[0]
    ns = seq // tc
    kern = functools.partial(_ret_kernel, tc=tc)
    return pl.pallas_call(
        kern,
        out_shape=jax.ShapeDtypeStruct((t, n_heads * HEAD_DIM), BF16),
        grid=(batch, n_heads, ns),
        in_specs=[pl.BlockSpec((tc, HEAD_DIM), lambda b, h, s: (b * ns + s, q_blk + h)),
                  pl.BlockSpec((tc, HEAD_DIM), lambda b, h, s: (b * ns + s, k_blk + h)),
                  pl.BlockSpec((tc, HEAD_DIM), lambda b, h, s: (b * ns + s, v_blk + h)),
                  pl.BlockSpec((tc, HEAD_DIM), lambda b, h, s: (b * ns + s, g_blk + h)),
                  pl.BlockSpec((tc, HEAD_DIM), lambda b, h, s: (s, 0)),
                  pl.BlockSpec((tc, HEAD_DIM), lambda b, h, s: (s, 0)),
                  pl.BlockSpec((None, 8, LANES), lambda b, h, s: (h, 0, 0))],
        out_specs=pl.BlockSpec((tc, HEAD_DIM), lambda b, h, s: (b * ns + s, h)),
        scratch_shapes=[pltpu.VMEM((HEAD_DIM, HEAD_DIM), F32)],
        compiler_params=_cparams(3), name="retention",
    )(p, p, p, p, cos2, sin2, lg_tab)


def _dn_kernel(uq_ref, uk_ref, uv_ref, z_ref, cwq_ref, cwk_ref, cwv_ref, sm_ref, smt_ref,
               crow_ref, ccol_ref, ng_ref, o_ref,
               state_ref, bq_ref, bk_ref, bv_ref, qs_ref, ks_ref, vs_ref, *, tr):
    c = DN_CHUNK
    g_heads = DN_GROUP
    gw = g_heads * HEAD_DIM

    @pl.when(pl.program_id(2) == 0)
    def _():
        state_ref[...] = jnp.zeros_like(state_ref)
        bq_ref[0:8, :] = jnp.zeros((8, gw), F32)
        bk_ref[0:8, :] = jnp.zeros((8, gw), F32)
        bv_ref[0:8, :] = jnp.zeros((8, gw), F32)

    def conv(u_ref, cw_ref, buf_ref, dst_ref):
        buf_ref[8:8 + tr, :] = u_ref[...].astype(F32)
        y = cw_ref[3:4, :] * buf_ref[8:8 + tr, :]
        for j in range(CONV_K - 1):
            y = y + cw_ref[j:j + 1, :] * buf_ref[5 + j:5 + j + tr, :]
        buf_ref[0:8, :] = buf_ref[tr:tr + 8, :]
        dst_ref[...] = _silu(y)

    conv(uq_ref, cwq_ref, bq_ref, qs_ref)
    conv(uk_ref, cwk_ref, bk_ref, ks_ref)
    conv(uv_ref, cwv_ref, bv_ref, vs_ref)

    sm = sm_ref[...]
    beta_all = _sigmoid(sm)
    g_all = crow_ref[1:2, :] * _softplus(sm + crow_ref[0:1, :])
    smt = smt_ref[...]
    g_rows = ccol_ref[1, :, 0:1] * _softplus(smt + ccol_ref[0, :, 0:1])

    rowi = lax.broadcasted_iota(jnp.int32, (c, c), 0)
    coli = lax.broadcasted_iota(jnp.int32, (c, c), 1)
    incl = rowi >= coli
    strict = rowi > coli
    tri_l = jnp.where(incl, 1.0, 0.0).astype(BF16)
    tri_u = jnp.where(rowi <= coli, 1.0, 0.0).astype(BF16)
    eye = jnp.where(rowi == coli, 1.0, 0.0).astype(F32)
    ng = ng_ref[...]

    for ci in range(tr // c):
        sl = slice(ci * c, (ci + 1) * c)
        gh, gm, gl = _split3(g_all[sl, :])
        gc_all = _dot(tri_l, gh) + _dot(tri_l, gm) + _dot(tri_l, gl)
        rh, rm, rl = _split3(g_rows[:, sl])
        gc_rows = _dot(rh, tri_u) + _dot(rm, tri_u) + _dot(rl, tri_u)
        for hh in range(g_heads):
            hs = slice(hh * HEAD_DIM, (hh + 1) * HEAD_DIM)
            q = qs_ref[sl, hs]
            k = ks_ref[sl, hs]
            v = vs_ref[sl, hs]
            q = q * lax.rsqrt(jnp.sum(q * q, axis=-1, keepdims=True) + EPS) * (HEAD_DIM ** -0.5)
            k = k * lax.rsqrt(jnp.sum(k * k, axis=-1, keepdims=True) + EPS)
            beta_c = beta_all[sl, hh:hh + 1]
            gc_c = gc_all[:, g_heads + hh:g_heads + hh + 1]
            gc_r = gc_rows[g_heads + hh:g_heads + hh + 1, :]
            gamma = jnp.exp(jnp.where(incl, gc_c - gc_r, -jnp.inf))
            kb = k.astype(BF16)
            kk = _dot_nt(kb, kb)
            qk = _dot_nt(q.astype(BF16), kb)
            a_mat = jnp.where(strict, beta_c * kk * gamma, 0.0)
            pw = -a_mat
            tinv = eye + pw
            for _ in range(5):
                pw = _dot_hi(pw, pw)
                tinv = tinv + _dot_hi(tinv, pw)
            e_gc = jnp.exp(gc_c)
            rhs = jnp.concatenate([k * (beta_c * e_gc), v * beta_c], axis=1)
            sol = _dot_hi(tinv, rhs)
            w_c = sol[:, :HEAD_DIM]
            u_c = sol[:, HEAD_DIM:]
            attn = qk * gamma
            gc_last = gc_c[c - 1:c, :]
            q_dec = q * e_gc
            k_dec = k * jnp.exp(gc_last - gc_c)
            st = state_ref[hh]
            stb = st.astype(BF16)
            v_new = u_c - _dot(w_c.astype(BF16), stb)
            vnb = v_new.astype(BF16)
            o = _dot(q_dec.astype(BF16), stb) + _dot(attn.astype(BF16), vnb)
            state_ref[hh] = st * jnp.exp(gc_last) + _dot_tn(k_dec.astype(BF16), vnb)
            o = _rms_rows(o) * ng * _silu(z_ref[sl, hs].astype(F32))
            o_ref[sl, hs] = o.astype(o_ref.dtype)


def deltanet_mix(p, conv_w8, small, small_t, crow, ccol, norm_g, batch, seq, n_heads,
                 q_blk, k_blk, v_blk, z_blk, tr=256):
    t = p.shape[0]
    ns = seq // tr
    ngroups = n_heads // DN_GROUP
    gw = DN_GROUP * HEAD_DIM
    kern = functools.partial(_dn_kernel, tr=tr)
    row = lambda b, g, s: b * ns + s
    return pl.pallas_call(
        kern,
        out_shape=jax.ShapeDtypeStruct((t, n_heads * HEAD_DIM), BF16),
        grid=(batch, ngroups, ns),
        in_specs=[pl.BlockSpec((tr, gw), lambda b, g, s: (row(b, g, s), q_blk + g)),
                  pl.BlockSpec((tr, gw), lambda b, g, s: (row(b, g, s), k_blk + g)),
                  pl.BlockSpec((tr, gw), lambda b, g, s: (row(b, g, s), v_blk + g)),
                  pl.BlockSpec((tr, gw), lambda b, g, s: (row(b, g, s), z_blk + g)),
                  pl.BlockSpec((8, gw), lambda b, g, s: (0, q_blk + g)),
                  pl.BlockSpec((8, gw), lambda b, g, s: (0, k_blk + g)),
                  pl.BlockSpec((8, gw), lambda b, g, s: (0, v_blk + g)),
                  pl.BlockSpec((tr, LANES), lambda b, g, s: (row(b, g, s), g)),
                  pl.BlockSpec((None, None, 8, tr), lambda b, g, s: (b, g, 0, s)),
                  pl.BlockSpec((None, 8, LANES), lambda b, g, s: (g, 0, 0)),
                  pl.BlockSpec((None, 2, 8, LANES), lambda b, g, s: (g, 0, 0, 0)),
                  pl.BlockSpec((1, HEAD_DIM), lambda b, g, s: (0, 0))],
        out_specs=pl.BlockSpec((tr, gw), lambda b, g, s: (row(b, g, s), g)),
        scratch_shapes=[pltpu.VMEM((DN_GROUP, HEAD_DIM, HEAD_DIM), F32),
                        pltpu.VMEM((tr + 8, gw), F32), pltpu.VMEM((tr + 8, gw), F32),
                        pltpu.VMEM((tr + 8, gw), F32),
                        pltpu.VMEM((tr, gw), F32), pltpu.VMEM((tr, gw), F32), pltpu.VMEM((tr, gw), F32)],
        compiler_params=_cparams(3), name="deltanet",
    )(p, p, p, p, conv_w8, conv_w8, conv_w8, small, small_t, crow, ccol, norm_g)


def _norm_router_kernel(x_ref, g_ref, sc_ref, sh_ref, wr_ref, br_ref, h_ref, lg_ref):
    x = x_ref[...]
    h = _rms_rows(x) * g_ref[...] * (1.0 + sc_ref[0]) + sh_ref[0]
    h_ref[...] = h
    w = wr_ref[...]
    w_hi = w.astype(BF16)
    w_lo = (w - w_hi.astype(F32)).astype(BF16)
    h_hi = h.astype(BF16)
    h_lo = (h - h_hi.astype(F32)).astype(BF16)
    lg_ref[...] = _dot(h_hi, w_hi) + _dot(h_lo, w_hi) + _dot(h_hi, w_lo) + br_ref[...]


def norm_router(x2d, g, sc, sh, w_router_pad, b_router_pad, seq, tm=256):
    t, d = x2d.shape
    tm = min(tm, seq)
    nb = seq // tm
    return pl.pallas_call(
        _norm_router_kernel,
        out_shape=(jax.ShapeDtypeStruct((t, d), F32), jax.ShapeDtypeStruct((t, LANES), F32)),
        grid=(t // tm,),
        in_specs=[pl.BlockSpec((tm, d), lambda i: (i, 0)),
                  pl.BlockSpec((1, d), lambda i: (0, 0)),
                  pl.BlockSpec((1, 1, d), lambda i: (i // nb, 0, 0)),
                  pl.BlockSpec((1, 1, d), lambda i: (i // nb, 0, 0)),
                  pl.BlockSpec((d, LANES), lambda i: (0, 0)),
                  pl.BlockSpec((1, LANES), lambda i: (0, 0))],
        out_specs=(pl.BlockSpec((tm, d), lambda i: (i, 0)),
                   pl.BlockSpec((tm, LANES), lambda i: (i, 0))),
        compiler_params=_cparams(1), name="norm_router",
    )(x2d, g.reshape(1, d), sc, sh, w_router_pad, b_router_pad)


def _moe_kernel(blk_e_ref, nused_ref, tok_ref, h_hbm, wg_ref, wl_ref, wd_ref, bg_ref, bl_ref, bd_ref,
                rw_ref, y_ref, xbuf, sem, *, tb):
    i = pl.program_id(0)

    def row_copy(r):
        return pltpu.make_async_copy(h_hbm.at[pl.ds(tok_ref[0, r], 1), :],
                                     xbuf.at[pl.ds(r, 1), :], sem.at[0])

    @pl.when(i < nused_ref[0])
    def _():
        def issue(r, carry):
            row_copy(r).start()
            return carry
        lax.fori_loop(0, tb, issue, 0)

        def drain(r, carry):
            row_copy(r).wait()
            return carry
        lax.fori_loop(0, tb, drain, 0)

        x = xbuf[...].astype(BF16)
        gate = jnp.minimum(_dot(x, wg_ref[...]) + bg_ref[...], SWIGLU_LIMIT)
        lin = jnp.clip(_dot(x, wl_ref[...]) + bl_ref[...], -SWIGLU_LIMIT, SWIGLU_LIMIT)
        act = gate * _sigmoid(SWIGLU_ALPHA * gate) * (lin + 1.0)
        y = _dot(act.astype(BF16), wd_ref[...]) + bd_ref[...]
        y_ref[...] = y * rw_ref[...]

    @pl.when(i >= nused_ref[0])
    def _():
        y_ref[...] = jnp.zeros_like(y_ref)


def moe_experts(h2, w_gate, w_lin, w_down, b_gate, b_lin, b_down, blk_e, n_used, row_tok, row_w, tb):
    t, d = h2.shape
    n_blk = row_tok.shape[0]
    f = w_gate.shape[2]
    kern = functools.partial(_moe_kernel, tb=tb)
    grid_spec = pltpu.PrefetchScalarGridSpec(
        num_scalar_prefetch=2,
        grid=(n_blk,),
        in_specs=[pl.BlockSpec((None, 1, tb), lambda i, be, nu: (i, 0, 0), memory_space=pltpu.SMEM),
                  pl.BlockSpec(memory_space=pl.ANY),
                  pl.BlockSpec((None, d, f), lambda i, be, nu: (be[i], 0, 0)),
                  pl.BlockSpec((None, d, f), lambda i, be, nu: (be[i], 0, 0)),
                  pl.BlockSpec((None, f, d), lambda i, be, nu: (be[i], 0, 0)),
                  pl.BlockSpec((None, 1, f), lambda i, be, nu: (be[i], 0, 0)),
                  pl.BlockSpec((None, 1, f), lambda i, be, nu: (be[i], 0, 0)),
                  pl.BlockSpec((None, 1, d), lambda i, be, nu: (be[i], 0, 0)),
                  pl.BlockSpec((tb, 1), lambda i, be, nu: (i, 0))],
        out_specs=pl.BlockSpec((tb, d), lambda i, be, nu: (i, 0)),
        scratch_shapes=[pltpu.VMEM((tb, d), F32), pltpu.SemaphoreType.DMA((1,))])
    return pl.pallas_call(
        kern,
        out_shape=jax.ShapeDtypeStruct((n_blk * tb, d), F32),
        grid_spec=grid_spec,
        compiler_params=_cparams(1), name="moe_experts",
    )(blk_e, n_used, row_tok, h2, w_gate, w_lin, w_down, b_gate, b_lin, b_down, row_w)


def _combine_kernel(pos_ref, y_hbm, x_ref, g2_ref, ng_ref, sc_ref, sh_ref, *rest, tm, final):
    if final:
        out_ref, ybuf, sem = rest
    else:
        xo_ref, h_ref, ybuf, sem = rest

    def row_copy(r, k):
        return pltpu.make_async_copy(y_hbm.at[pl.ds(pos_ref[0, r * TOP_K + k], 1), :],
                                     ybuf.at[k, pl.ds(r, 1), :], sem.at[0])

    def issue(r, carry):
        for k in range(TOP_K):
            row_copy(r, k).start()
        return carry
    lax.fori_loop(0, tm, issue, 0)

    def drain(r, carry):
        for k in range(TOP_K):
            row_copy(r, k).wait()
        return carry
    lax.fori_loop(0, tm, drain, 0)

    moe = (ybuf[0] + ybuf[1]) + (ybuf[2] + ybuf[3])
    x = x_ref[...] + g2_ref[0] * moe
    if final:
        out_ref[...] = _rms_rows(x) * ng_ref[...]
    else:
        xo_ref[...] = x
        h = _rms_rows(x) * ng_ref[...] * (1.0 + sc_ref[0]) + sh_ref[0]
        h_ref[...] = h.astype(h_ref.dtype)


def moe_combine(y, pos, x2d, gate2, norm_g, sc, sh, seq, final, tm=128):
    t, d = x2d.shape
    tm = min(tm, seq)
    nb = seq // tm
    kern = functools.partial(_combine_kernel, tm=tm, final=final)
    if final:
        out_shape = jax.ShapeDtypeStruct((t, d), F32)
        out_specs = pl.BlockSpec((tm, d), lambda i: (i, 0))
    else:
        out_shape = (jax.ShapeDtypeStruct((t, d), F32), jax.ShapeDtypeStruct((t, d), BF16))
        out_specs = (pl.BlockSpec((tm, d), lambda i: (i, 0)), pl.BlockSpec((tm, d), lambda i: (i, 0)))
    return pl.pallas_call(
        kern,
        out_shape=out_shape,
        grid=(t // tm,),
        in_specs=[pl.BlockSpec((None, 1, tm * TOP_K), lambda i: (i, 0, 0), memory_space=pltpu.SMEM),
                  pl.BlockSpec(memory_space=pl.ANY),
                  pl.BlockSpec((tm, d), lambda i: (i, 0)),
                  pl.BlockSpec((1, 1, d), lambda i: (i // nb, 0, 0)),
                  pl.BlockSpec((1, d), lambda i: (0, 0)),
                  pl.BlockSpec((1, 1, d), lambda i: (i // nb, 0, 0)),
                  pl.BlockSpec((1, 1, d), lambda i: (i // nb, 0, 0))],
        out_specs=out_specs,
        scratch_shapes=[pltpu.VMEM((TOP_K, tm, d), F32), pltpu.SemaphoreType.DMA((1,))],
        compiler_params=_cparams(1), name="moe_combine",
    )(pos.reshape(t // tm, 1, tm * TOP_K), y, x2d, gate2, norm_g.reshape(1, d), sc, sh)


def _routing_tables(logits, n_experts, tb):
    t = logits.shape[0]
    top_v, top_e = lax.top_k(logits[:, :n_experts], TOP_K)
    top_w = jax.nn.softmax(top_v, axis=-1)
    n_a = t * TOP_K
    e_a = top_e.reshape(n_a)
    onehot = (e_a[:, None] == jnp.arange(n_experts, dtype=e_a.dtype)[None, :]).astype(jnp.int32)
    csum = jnp.cumsum(onehot, axis=0)
    counts = csum[-1]
    rank = jnp.sum(jnp.where(onehot > 0, csum, 0), axis=1) - 1
    padded = (counts + tb - 1) // tb * tb
    pend = jnp.cumsum(padded)
    pstart = pend - padded
    pos = (pstart[e_a] + rank).astype(jnp.int32)
    n_rows = (n_a + n_experts * (tb - 1) + tb - 1) // tb * tb
    n_blk = n_rows // tb
    t_a = jnp.repeat(jnp.arange(t, dtype=jnp.int32), TOP_K)
    row_tok = jnp.zeros((n_rows,), jnp.int32).at[pos].set(t_a)
    row_w = jnp.zeros((n_rows,), F32).at[pos].set(top_w.reshape(n_a))
    blk_e = jnp.minimum(jnp.searchsorted(pend, jnp.arange(n_blk, dtype=jnp.int32) * tb, side='right'),
                        n_experts - 1).astype(jnp.int32)
    n_used = (pend[-1] // tb).astype(jnp.int32).reshape(1)
    return pos, row_tok.reshape(n_blk, 1, tb), row_w.reshape(n_rows, 1), blk_e, n_used


def _rope_tables(seq):
    half = HEAD_DIM // 2
    inv = ROPE_BASE ** (-jnp.arange(half, dtype=F32) / half)
    ang = jnp.arange(seq, dtype=F32)[:, None] * inv[None, :]
    cos, sin = jnp.cos(ang), jnp.sin(ang)
    return jnp.concatenate([cos, cos], axis=-1), jnp.concatenate([-sin, sin], axis=-1)


def kernel(x, c, norm1_g, w_ada, b_ada, w_in, dn_conv_w, dn_a_log, dn_dt_bias, dn_norm_g, fox_f_bias,
           w_out, norm2_g, w_router, b_router, w_gu, b_gu, w_down, b_down, final_norm_g):
    batch, seq, d = x.shape
    depth = w_ada.shape[0]
    t = batch * seq
    dn_heads = dn_a_log.shape[1]
    fox_heads = fox_f_bias.shape[1]
    ret_heads = d // HEAD_DIM - dn_heads - fox_heads
    dn_w, ret_w, fox_w = dn_heads * HEAD_DIM, ret_heads * HEAD_DIM, fox_heads * HEAD_DIM
    n_experts = w_router.shape[2]
    n_groups = dn_heads // DN_GROUP
    gw = DN_GROUP * HEAD_DIM
    moe_tb = 256

    widths = (3 * dn_w, dn_w, dn_heads, dn_heads, ret_w, ret_w, ret_w, ret_w,
              fox_w, fox_w, fox_w, fox_w, fox_heads)
    cuts = np.concatenate([[0], np.cumsum(widths)])
    o_b, o_a, o_r, o_ff = int(cuts[2]), int(cuts[3]), int(cuts[4]), int(cuts[12])
    small_src, small_dst = [], []
    for g in range(n_groups):
        for hh in range(DN_GROUP):
            small_src += [o_b + g * DN_GROUP + hh, o_a + g * DN_GROUP + hh]
            small_dst += [g * LANES + hh, g * LANES + DN_GROUP + hh]
    for hh in range(fox_heads):
        small_src.append(o_ff + hh)
        small_dst.append(n_groups * LANES + hh)
    small_src = np.asarray(small_src, np.int32)
    small_dst = np.asarray(small_dst, np.int32)

    x2d = x.reshape(t, d)
    c_pad = jnp.zeros((8, d), BF16).at[:batch].set(c.astype(BF16))
    mod = ada_mod(c_pad, w_ada, b_ada)[:, :batch, :]
    mod = mod.reshape(depth, batch, 6, 1, d)
    cos2, sin2 = _rope_tables(seq)
    lg_tab = jnp.log(1.0 - 2.0 ** (-5.0 - jnp.arange(ret_heads, dtype=F32)))
    lg_tab = jnp.broadcast_to(lg_tab[:, None, None], (ret_heads, 8, LANES))

    dn_q_blk, dn_k_blk, dn_v_blk, dn_z_blk = 0, dn_w // gw, 2 * dn_w // gw, 3 * dn_w // gw
    r0 = 4 * dn_w // HEAD_DIM
    r_q_blk, r_k_blk, r_v_blk, r_g_blk = r0, r0 + ret_heads, r0 + 2 * ret_heads, r0 + 3 * ret_heads
    f0 = r0 + 4 * ret_heads
    f_q_blk, f_k_blk, f_v_blk, f_z_blk = f0, f0 + fox_heads, f0 + 2 * fox_heads, f0 + 3 * fox_heads

    h = norm_mod(x2d, norm1_g[0], mod[0, :, 1], mod[0, :, 0], seq)
    out = None
    for l in range(depth):
        sh1, sc1, g1, sh2, sc2, g2 = (mod[l, :, i] for i in range(6))
        w_main = jnp.concatenate([w_in[l][:, :o_b], w_in[l][:, o_r:o_ff]], axis=1).astype(BF16)
        w_small = jnp.zeros((d, (n_groups + 1) * LANES), BF16).at[:, small_dst].set(
            w_in[l][:, small_src].astype(BF16))
        p = matmul_bf16(h, w_main, BF16)
        small = matmul_bf16(h, w_small, F32, tn=(n_groups + 1) * LANES)

        conv_w8 = jnp.zeros((8, 3 * dn_w), F32).at[:CONV_K].set(dn_conv_w[l])
        small_t = small[:, :n_groups * LANES].reshape(batch, seq, n_groups, LANES)[..., :8]
        small_t = small_t.transpose(0, 2, 3, 1)
        dtb = dn_dt_bias[l].reshape(n_groups, DN_GROUP)
        nega = -jnp.exp(dn_a_log[l]).reshape(n_groups, DN_GROUP)
        zg = jnp.zeros((n_groups, DN_GROUP), F32)
        pad = jnp.zeros((n_groups, LANES - 2 * DN_GROUP), F32)
        crow = jnp.stack([jnp.concatenate([zg, dtb, pad], axis=1),
                          jnp.concatenate([zg, nega, pad], axis=1)], axis=1)
        crow = jnp.concatenate([crow, jnp.zeros((n_groups, 6, LANES), F32)], axis=1)
        ccol = jnp.stack([jnp.concatenate([zg, dtb], axis=1), jnp.concatenate([zg, nega], axis=1)], axis=1)
        ccol = jnp.broadcast_to(ccol[..., None], (n_groups, 2, 8, LANES))
        o_dn = deltanet_mix(p, conv_w8, small, small_t, crow, ccol, dn_norm_g[l].reshape(1, HEAD_DIM),
                            batch, seq, dn_heads, dn_q_blk, dn_k_blk, dn_v_blk, dn_z_blk)
        o_ret = retention_mix(p, cos2, sin2, lg_tab, batch, seq, ret_heads,
                              r_q_blk, r_k_blk, r_v_blk, r_g_blk)
        fb = jnp.zeros((1, LANES), F32).at[0, :fox_heads].set(fox_f_bias[l])
        f_cum = fox_prefix(small, fb, batch, seq, n_groups)
        f_cum_t = f_cum.reshape(batch, seq, LANES)[..., :16].transpose(0, 2, 1)
        o_fox = fox_attention(p, f_cum, f_cum_t, batch, seq, fox_heads,
                              f_q_blk, f_k_blk, f_v_blk, f_z_blk)
        o_mix = jnp.concatenate([o_dn, o_ret, o_fox], axis=-1)
        x2d = out_proj_resid(o_mix, w_out[l], x2d, g1, seq)

        wr_pad = jnp.zeros((d, LANES), F32).at[:, :n_experts].set(w_router[l])
        br_pad = jnp.zeros((1, LANES), F32).at[0, :n_experts].set(b_router[l])
        h2, logits = norm_router(x2d, norm2_g[l], sc2, sh2, wr_pad, br_pad, seq)
        pos, row_tok, row_w, blk_e, n_used = _routing_tables(logits, n_experts, moe_tb)
        w_gate = w_gu[l][:, :, 0::2].astype(BF16)
        w_lin = w_gu[l][:, :, 1::2].astype(BF16)
        b_gate = b_gu[l][:, None, 0::2]
        b_lin = b_gu[l][:, None, 1::2]
        y = moe_experts(h2, w_gate, w_lin, w_down[l].astype(BF16), b_gate, b_lin, b_down[l][:, None, :],
                        blk_e, n_used, row_tok, row_w, moe_tb)
        if l + 1 < depth:
            nsh1, nsc1 = mod[l + 1, :, 0], mod[l + 1, :, 1]
            x2d, h = moe_combine(y, pos, x2d, g2, norm1_g[l + 1], nsc1, nsh1, seq, final=False)
        else:
            out = moe_combine(y, pos, x2d, g2, final_norm_g, sc2, sh2, seq, final=True)
    return out.reshape(batch, seq, d)
```

```python
import functools

import numpy as np
import jax
import jax.numpy as jnp
from jax import lax
from jax.experimental import pallas as pl
from jax.experimental.pallas import tpu as pltpu

F32 = jnp.float32
BF16 = jnp.bfloat16
U32 = jnp.uint32

HEAD_DIM = 128
CONV_K = 4
DN_CHUNK = 64
RET_CHUNK = 128
ROPE_BASE = 10000.0
TOP_K = 4
SWIGLU_LIMIT = 7.0
SWIGLU_ALPHA = 1.702
EPS = 1e-6
DN_GROUP = 4
LANES = 128
VMEM_LIMIT = 56 * 1024 * 1024
LOG2E = 1.4426950408889634
HI16 = 0xFFFF0000


def _cparams(n_axes):
    return pltpu.CompilerParams(dimension_semantics=("arbitrary",) * n_axes,
                                vmem_limit_bytes=VMEM_LIMIT)


def _dot(a, b):
    return jnp.dot(a, b, preferred_element_type=F32)


def _dot_nt(a, b):
    return lax.dot_general(a, b, (((1,), (1,)), ((), ())), preferred_element_type=F32)


def _dot_tn(a, b):
    return lax.dot_general(a, b, (((0,), (0,)), ((), ())), preferred_element_type=F32)


def _split3(x):
    h = x.astype(BF16)
    r = x - h.astype(F32)
    m = r.astype(BF16)
    l = (r - m.astype(F32)).astype(BF16)
    return h, m, l


def _sigmoid(x):
    return 1.0 / (1.0 + jnp.exp(-x))


def _silu(x):
    return x * _sigmoid(x)


def _softplus(x):
    return jnp.maximum(x, 0.0) + jnp.log1p(jnp.exp(-jnp.abs(x)))


def _log_sigmoid(x):
    return jnp.minimum(x, 0.0) - jnp.log1p(jnp.exp(-jnp.abs(x)))


def _rms_rows(x):
    return x * lax.rsqrt(jnp.mean(x * x, axis=-1, keepdims=True) + EPS)


def _bf16_bits_lo(x):
    return pltpu.bitcast(x.astype(BF16).astype(F32), U32) >> 16


def _bf16_bits_hi(x):
    return pltpu.bitcast(x.astype(BF16).astype(F32), U32) & jnp.uint32(HI16)


def _ada_kernel(c_ref, w_ref, b_ref, o_ref):
    o_ref[...] = _dot(c_ref[...], w_ref[...].astype(BF16)) + b_ref[...]


def ada_mod(c_pad, w_ada, b_ada, tn=512):
    depth, d, n = w_ada.shape
    return pl.pallas_call(
        _ada_kernel,
        out_shape=jax.ShapeDtypeStruct((depth, 8, n), F32),
        grid=(depth, n // tn),
        in_specs=[pl.BlockSpec((8, d), lambda l, j: (0, 0)),
                  pl.BlockSpec((None, d, tn), lambda l, j: (l, 0, j)),
                  pl.BlockSpec((None, 1, tn), lambda l, j: (l, 0, j))],
        out_specs=pl.BlockSpec((None, 8, tn), lambda l, j: (l, 0, j)),
        compiler_params=_cparams(2), name="ada_mod",
    )(c_pad, w_ada, b_ada.reshape(depth, 1, n))


def _norm_mod_kernel(x_ref, g_ref, sc_ref, sh_ref, h_ref):
    x = x_ref[...]
    h = _rms_rows(x) * g_ref[...] * (1.0 + sc_ref[0]) + sh_ref[0]
    h_ref[...] = h.astype(h_ref.dtype)


def norm_mod(x2d, g, sc, sh, seq, tm=256):
    t, d = x2d.shape
    tm = min(tm, seq)
    nb = seq // tm
    return pl.pallas_call(
        _norm_mod_kernel,
        out_shape=jax.ShapeDtypeStruct((t, d), BF16),
        grid=(t // tm,),
        in_specs=[pl.BlockSpec((tm, d), lambda i: (i, 0)),
                  pl.BlockSpec((1, d), lambda i: (0, 0)),
                  pl.BlockSpec((1, 1, d), lambda i: (i // nb, 0, 0)),
                  pl.BlockSpec((1, 1, d), lambda i: (i // nb, 0, 0))],
        out_specs=pl.BlockSpec((tm, d), lambda i: (i, 0)),
        compiler_params=_cparams(1), name="norm_mod",
    )(x2d, g.reshape(1, d), sc, sh)


def _mm_kernel(x_ref, w_ref, o_ref):
    o_ref[...] = _dot(x_ref[...], w_ref[...]).astype(o_ref.dtype)


def matmul_bf16(x, w, out_dtype, tm=1024):
    m, k = x.shape
    n = w.shape[1]
    tm = min(tm, m)
    return pl.pallas_call(
        _mm_kernel,
        out_shape=jax.ShapeDtypeStruct((m, n), out_dtype),
        grid=(m // tm,),
        in_specs=[pl.BlockSpec((tm, k), lambda i: (i, 0)),
                  pl.BlockSpec((k, n), lambda i: (0, 0))],
        out_specs=pl.BlockSpec((tm, n), lambda i: (i, 0)),
        compiler_params=_cparams(1), name="small_proj",
    )(x, w)


def _inproj_kernel(x_ref, w_ref, wt_ref, o_ref, wb_ref, *, shift, tn, n_win):
    j = pl.program_id(0)

    @pl.when((pl.program_id(1) == 0) & (j < n_win))
    def _():
        k = wb_ref.shape[0]
        rows = min(256, k)

        def body(r, carry):
            r0 = pl.multiple_of(r * rows, rows)
            wb_ref[pl.ds(r0, rows), :] = w_ref[0, pl.ds(r0, rows), shift:shift + tn].astype(BF16)
            return carry
        lax.fori_loop(0, k // rows, body, 0)

    @pl.when((pl.program_id(1) == 0) & (j >= n_win))
    def _():
        wb_ref[...] = wt_ref[...]

    o_ref[...] = _dot(x_ref[...], wb_ref[...]).astype(o_ref.dtype)


def in_proj(h, w_in, layer, col0, n_win, w_tail, tm=1024, tn=512):
    m, k = h.shape
    tm = min(tm, m)
    shift = col0 % LANES
    base = col0 - shift
    win = tn + (LANES if shift else 0)
    kern = functools.partial(_inproj_kernel, shift=shift, tn=tn, n_win=n_win)
    return pl.pallas_call(
        kern,
        out_shape=jax.ShapeDtypeStruct((m, (n_win + 1) * tn), BF16),
        grid=(n_win + 1, m // tm),
        in_specs=[pl.BlockSpec((tm, k), lambda j, i: (i, 0)),
                  pl.BlockSpec((pl.Element(1), pl.Element(k), pl.Element(win)),
                               lambda j, i: (layer, 0, pl.multiple_of(
                                   base + jnp.minimum(j, n_win - 1) * tn, LANES))),
                  pl.BlockSpec((k, tn), lambda j, i: (0, 0))],
        out_specs=pl.BlockSpec((tm, tn), lambda j, i: (i, j)),
        scratch_shapes=[pltpu.VMEM((k, tn), BF16)],
        compiler_params=_cparams(2), name="in_proj",
    )(h, w_in, w_tail)


def _mm_resid_kernel(a_ref, b_ref, c_ref, w_ref, x_ref, g_ref, o_ref, wb_ref, *, splits):
    @pl.when(pl.program_id(1) == 0)
    def _():
        wb_ref[...] = w_ref[...].astype(BF16)
    k0, k1 = splits
    y = (_dot(a_ref[...], wb_ref[0:k0, :]) + _dot(b_ref[...], wb_ref[k0:k1, :])
         + _dot(c_ref[...], wb_ref[k1:, :]))
    o_ref[...] = x_ref[...] + g_ref[0] * y


def out_proj_resid(a, b, c, w_out, layer, x2d, gate, seq, tm=1024, tn=512):
    m = a.shape[0]
    ka, kb, kc = a.shape[1], b.shape[1], c.shape[1]
    k = ka + kb + kc
    n = w_out.shape[2]
    tm = min(tm, seq)
    nb = seq // tm
    kern = functools.partial(_mm_resid_kernel, splits=(ka, ka + kb))
    return pl.pallas_call(
        kern,
        out_shape=jax.ShapeDtypeStruct((m, n), F32),
        grid=(n // tn, m // tm),
        in_specs=[pl.BlockSpec((tm, ka), lambda j, i: (i, 0)),
                  pl.BlockSpec((tm, kb), lambda j, i: (i, 0)),
                  pl.BlockSpec((tm, kc), lambda j, i: (i, 0)),
                  pl.BlockSpec((None, k, tn), lambda j, i: (layer, 0, j)),
                  pl.BlockSpec((tm, tn), lambda j, i: (i, j)),
                  pl.BlockSpec((1, 1, tn), lambda j, i: (i // nb, 0, j))],
        out_specs=pl.BlockSpec((tm, tn), lambda j, i: (i, j)),
        scratch_shapes=[pltpu.VMEM((k, tn), BF16)],
        compiler_params=_cparams(2), name="out_proj",
    )(a, b, c, w_out, x2d, gate)


def _fprep_kernel(s_ref, b_ref, f_ref, carry_ref):
    @pl.when(pl.program_id(1) == 0)
    def _():
        carry_ref[...] = jnp.zeros_like(carry_ref)
    tq = s_ref.shape[0]
    lf = _log_sigmoid(s_ref[...] + b_ref[...])
    row = lax.broadcasted_iota(jnp.int32, (tq, tq), 0)
    col = lax.broadcasted_iota(jnp.int32, (tq, tq), 1)
    tri = jnp.where(row >= col, 1.0, 0.0).astype(BF16)
    h, m, l = _split3(lf)
    f = _dot(tri, h) + _dot(tri, m) + _dot(tri, l) + carry_ref[0:1, :]
    f_ref[...] = f
    carry_ref[...] = jnp.broadcast_to(f[tq - 1:tq, :], carry_ref.shape)


def fox_prefix(small, bias_row, batch, seq, col_block, tq=512):
    t = small.shape[0]
    tq = min(tq, seq)
    nq = seq // tq
    return pl.pallas_call(
        _fprep_kernel,
        out_shape=jax.ShapeDtypeStruct((t, LANES), F32),
        grid=(batch, nq),
        in_specs=[pl.BlockSpec((tq, LANES), lambda b, i: (b * nq + i, col_block)),
                  pl.BlockSpec((1, LANES), lambda b, i: (0, 0))],
        out_specs=pl.BlockSpec((tq, LANES), lambda b, i: (b * nq + i, 0)),
        scratch_shapes=[pltpu.VMEM((8, LANES), F32)],
        compiler_params=_cparams(2), name="fox_prefix",
    )(small, bias_row)


def _fox_kernel(q_ref, k_ref, v_ref, z_ref, f_ref, o_ref, ka_ref, va_ref, m_ref, acc_ref, *, tq, seq):
    h = pl.program_id(1)
    qi = pl.program_id(2)
    lane = lax.broadcasted_iota(jnp.int32, (tq, LANES), 1)

    def f_column(rows):
        fb = f_ref[pl.ds(rows, tq), :]
        return jnp.sum(jnp.where(lane == h, fb, 0.0), axis=-1, keepdims=True) * LOG2E

    def aug_lanes(fcol, sign_first):
        fh, fm, fl = _split3(fcol)
        fh, fm, fl = fh.astype(F32), fm.astype(F32), fl.astype(F32)
        one = jnp.ones((tq, LANES), F32)
        zero = jnp.zeros((tq, LANES), F32)
        if sign_first:
            a = jnp.where(lane == 0, fh, jnp.where(lane == 1, fm, jnp.where(lane == 2, fl,
                          jnp.where(lane < 6, one, zero))))
        else:
            a = jnp.where(lane < 3, one, jnp.where(lane == 3, -fh, jnp.where(lane == 4, -fm,
                          jnp.where(lane == 5, -fl, zero))))
        return a.astype(BF16)

    @pl.when(qi == 0)
    def _():
        def build(j, carry):
            r0 = pl.multiple_of(j * tq, tq)
            ka_ref[pl.ds(r0, tq), 0:HEAD_DIM] = k_ref[pl.ds(r0, tq), :]
            ka_ref[pl.ds(r0, tq), HEAD_DIM:] = aug_lanes(f_column(r0), False)
            va_ref[pl.ds(r0, tq), 0:HEAD_DIM] = v_ref[pl.ds(r0, tq), :]
            va_ref[pl.ds(r0, tq), HEAD_DIM:] = jnp.ones((tq, HEAD_DIM), BF16)
            return carry
        lax.fori_loop(0, seq // tq, build, 0)

    q0 = pl.multiple_of(qi * tq, tq)
    qs = (q_ref[...].astype(F32) * (HEAD_DIM ** -0.5 * LOG2E)).astype(BF16)
    qa = jnp.concatenate([qs, aug_lanes(f_column(q0), True)], axis=1)

    m_ref[...] = jnp.full_like(m_ref, -1e30)
    acc_ref[...] = jnp.zeros_like(acc_ref)

    def block(k0, width, masked):
        s = _dot_nt(qa, ka_ref[pl.ds(k0, width), :])
        if masked:
            row = lax.broadcasted_iota(jnp.int32, (tq, width), 0)
            col = lax.broadcasted_iota(jnp.int32, (tq, width), 1)
            s = jnp.where(row >= col, s, -1e30)
        m_old = m_ref[...]
        m_new = jnp.maximum(m_old, jnp.max(s, axis=-1, keepdims=True))
        p = jnp.exp2(s - m_new).astype(BF16)
        acc_ref[...] = jnp.exp2(m_old - m_new) * acc_ref[...] + _dot(p, va_ref[pl.ds(k0, width), :])
        m_ref[...] = m_new

    def wide(j, carry):
        block(pl.multiple_of(j * (2 * tq), 2 * tq), 2 * tq, False)
        return carry
    lax.fori_loop(0, qi // 2, wide, 0)

    @pl.when(qi % 2 == 1)
    def _():
        block(pl.multiple_of((qi - 1) * tq, tq), tq, False)

    block(q0, tq, True)
    acc = acc_ref[...]
    o = acc[:, :HEAD_DIM] / acc[:, HEAD_DIM:]
    o_ref[...] = (o * _sigmoid(z_ref[...].astype(F32))).astype(o_ref.dtype)


def fox_attention(p, f_cum, batch, seq, n_heads, q_blk, k_blk, v_blk, z_blk, tq=512):
    t = p.shape[0]
    tq = min(tq, seq)
    nq = seq // tq
    kern = functools.partial(_fox_kernel, tq=tq, seq=seq)
    return pl.pallas_call(
        kern,
        out_shape=jax.ShapeDtypeStruct((t, n_heads * HEAD_DIM), BF16),
        grid=(batch, n_heads, nq),
        in_specs=[pl.BlockSpec((tq, HEAD_DIM), lambda b, h, i: (b * nq + i, q_blk + h)),
                  pl.BlockSpec((seq, HEAD_DIM), lambda b, h, i: (b, k_blk + h)),
                  pl.BlockSpec((seq, HEAD_DIM), lambda b, h, i: (b, v_blk + h)),
                  pl.BlockSpec((tq, HEAD_DIM), lambda b, h, i: (b * nq + i, z_blk + h)),
                  pl.BlockSpec((seq, LANES), lambda b, h, i: (b, 0))],
        out_specs=pl.BlockSpec((tq, HEAD_DIM), lambda b, h, i: (b * nq + i, h)),
        scratch_shapes=[pltpu.VMEM((seq, 2 * HEAD_DIM), BF16), pltpu.VMEM((seq, 2 * HEAD_DIM), BF16),
                        pltpu.VMEM((tq, 1), F32), pltpu.VMEM((tq, 2 * HEAD_DIM), F32)],
        compiler_params=_cparams(3), name="fox_attn",
    )(p, p, p, p, f_cum)


def _ret_kernel(q_ref, k_ref, v_ref, g_ref, cos_ref, sin_ref, lg_ref, o_ref, r_ref, *, tc):
    c = RET_CHUNK

    @pl.when(pl.program_id(2) == 0)
    def _():
        r_ref[...] = jnp.zeros_like(r_ref)

    lg = lg_ref[0:1, 0:1]
    rowi = lax.broadcasted_iota(jnp.int32, (c, c), 0)
    coli = lax.broadcasted_iota(jnp.int32, (c, c), 1)
    diff = (rowi - coli).astype(F32)
    decay = jnp.where(diff >= 0, jnp.exp(lg * jnp.maximum(diff, 0.0)), 0.0)
    pos = lax.broadcasted_iota(jnp.int32, (c, 1), 0).astype(F32)
    q_scale = jnp.exp(lg * (pos + 1.0))
    k_scale = jnp.exp(lg * (c - 1.0 - pos))
    chunk_decay = jnp.exp(lg * c)

    def rope(t, cos2, sin2):
        return t * cos2 + pltpu.roll(t, HEAD_DIM // 2, 1) * sin2

    for ci in range(tc // c):
        sl = slice(ci * c, (ci + 1) * c)
        cos2 = cos_ref[sl, :]
        sin2 = sin_ref[sl, :]
        q = rope(q_ref[sl, :].astype(F32), cos2, sin2)
        k = rope(k_ref[sl, :].astype(F32), cos2, sin2) * (HEAD_DIM ** -0.5)
        v = v_ref[sl, :]
        inner = _dot_nt(q.astype(BF16), k.astype(BF16)) * decay
        r_prev = r_ref[...]
        o = _dot(inner.astype(BF16), v) + _dot((q * q_scale).astype(BF16), r_prev.astype(BF16))
        r_ref[...] = r_prev * chunk_decay + _dot_tn((k * k_scale).astype(BF16), v)
        o = _rms_rows(o) * _silu(g_ref[sl, :].astype(F32))
        o_ref[sl, :] = o.astype(o_ref.dtype)


def retention_mix(p, cos2, sin2, lg_tab, batch, seq, n_heads, q_blk, k_blk, v_blk, g_blk, tc=512):
    t = p.shape[0]
    tc = min(tc, seq)
    ns = seq // tc
    kern = functools.partial(_ret_kernel, tc=tc)
    return pl.pallas_call(
        kern,
        out_shape=jax.ShapeDtypeStruct((t, n_heads * HEAD_DIM), BF16),
        grid=(batch, n_heads, ns),
        in_specs=[pl.BlockSpec((tc, HEAD_DIM), lambda b, h, s: (b * ns + s, q_blk + h)),
                  pl.BlockSpec((tc, HEAD_DIM), lambda b, h, s: (b * ns + s, k_blk + h)),
                  pl.BlockSpec((tc, HEAD_DIM), lambda b, h, s: (b * ns + s, v_blk + h)),
                  pl.BlockSpec((tc, HEAD_DIM), lambda b, h, s: (b * ns + s, g_blk + h)),
                  pl.BlockSpec((tc, HEAD_DIM), lambda b, h, s: (s, 0)),
                  pl.BlockSpec((tc, HEAD_DIM), lambda b, h, s: (s, 0)),
                  pl.BlockSpec((None, 8, LANES), lambda b, h, s: (h, 0, 0))],
        out_specs=pl.BlockSpec((tc, HEAD_DIM), lambda b, h, s: (b * ns + s, h)),
        scratch_shapes=[pltpu.VMEM((HEAD_DIM, HEAD_DIM), F32)],
        compiler_params=_cparams(3), name="retention",
    )(p, p, p, p, cos2, sin2, lg_tab)


def _dn_kernel(uq_ref, uk_ref, uv_ref, z_ref, cwq_ref, cwk_ref, cwv_ref, sm_ref, smt_ref,
               crow_ref, ccol_ref, ng_ref, o_ref,
               state_ref, bq_ref, bk_ref, bv_ref, qs_ref, ks_ref, vs_ref, *, tr):
    c = DN_CHUNK
    g_heads = DN_GROUP
    gw = g_heads * HEAD_DIM

    @pl.when(pl.program_id(2) == 0)
    def _():
        state_ref[...] = jnp.zeros_like(state_ref)
        bq_ref[0:8, :] = jnp.zeros((8, gw), F32)
        bk_ref[0:8, :] = jnp.zeros((8, gw), F32)
        bv_ref[0:8, :] = jnp.zeros((8, gw), F32)

    def conv(u_ref, cw_ref, buf_ref, dst_ref):
        buf_ref[8:8 + tr, :] = u_ref[...].astype(F32)
        y = cw_ref[3:4, :] * buf_ref[8:8 + tr, :]
        for j in range(CONV_K - 1):
            y = y + cw_ref[j:j + 1, :] * buf_ref[5 + j:5 + j + tr, :]
        buf_ref[0:8, :] = buf_ref[tr:tr + 8, :]
        dst_ref[...] = _silu(y)

    conv(uq_ref, cwq_ref, bq_ref, qs_ref)
    conv(uk_ref, cwk_ref, bk_ref, ks_ref)
    conv(uv_ref, cwv_ref, bv_ref, vs_ref)

    sm = sm_ref[...]
    beta_all = _sigmoid(sm)
    g_all = crow_ref[1:2, :] * _softplus(sm + crow_ref[0:1, :])
    smt = smt_ref[...]
    g_rows = ccol_ref[1, :, 0:1] * _softplus(smt + ccol_ref[0, :, 0:1])

    rowi = lax.broadcasted_iota(jnp.int32, (c, c), 0)
    coli = lax.broadcasted_iota(jnp.int32, (c, c), 1)
    incl = rowi >= coli
    strict = rowi > coli
    tri_l = jnp.where(incl, 1.0, 0.0).astype(BF16)
    tri_u = jnp.where(rowi <= coli, 1.0, 0.0).astype(BF16)
    ng = ng_ref[...]
    n_chunks = tr // c
    items = [(ci, hh) for ci in range(n_chunks) for hh in range(g_heads)]

    gc_cols, gc_rowsl = [], []
    for ci in range(n_chunks):
        sl = slice(ci * c, (ci + 1) * c)
        gh, gm, gl = _split3(g_all[sl, :])
        gc_cols.append(_dot(tri_l, gh) + _dot(tri_l, gm) + _dot(tri_l, gl))
        rh, rm, rl = _split3(g_rows[:, sl])
        gc_rowsl.append(_dot(rh, tri_u) + _dot(rm, tri_u) + _dot(rl, tri_u))

    qn, kn, vv, beta, gcc, gamma, qk, xm = {}, {}, {}, {}, {}, {}, {}, {}
    for it in items:
        ci, hh = it
        sl = slice(ci * c, (ci + 1) * c)
        hs = slice(hh * HEAD_DIM, (hh + 1) * HEAD_DIM)
        q = qs_ref[sl, hs]
        k = ks_ref[sl, hs]
        vv[it] = vs_ref[sl, hs]
        q = q * lax.rsqrt(jnp.sum(q * q, axis=-1, keepdims=True) + EPS) * (HEAD_DIM ** -0.5)
        k = k * lax.rsqrt(jnp.sum(k * k, axis=-1, keepdims=True) + EPS)
        qn[it], kn[it] = q, k
        beta[it] = beta_all[sl, hh:hh + 1]
        gcc[it] = gc_cols[ci][:, g_heads + hh:g_heads + hh + 1]
        gc_r = gc_rowsl[ci][g_heads + hh:g_heads + hh + 1, :]
        gamma[it] = jnp.exp(jnp.where(incl, gcc[it] - gc_r, -jnp.inf))
        kb = k.astype(BF16)
        qkk = _dot_nt(jnp.concatenate([q.astype(BF16), kb], axis=0), kb)
        qk[it] = qkk[:c]
        xm[it] = -jnp.where(strict, beta[it] * qkk[c:] * gamma[it], 0.0)

    blk_r = [rowi // b for b in (8, 16, 32)]
    blk_c = [coli // b for b in (8, 16, 32)]
    same8 = blk_r[0] == blk_c[0]
    nm = {}
    for it in items:
        x8 = jnp.where(same8, xm[it], 0.0)
        xb = x8.astype(BF16)
        p1 = _dot(xb, xb)
        r = _dot(jnp.concatenate([x8, p1], axis=0).astype(BF16), p1.astype(BF16))
        n3 = x8 + p1 + r[:c]
        p2 = r[c:]
        nm[it] = n3 + p2 + _dot(n3.astype(BF16), p2.astype(BF16))
    for lvl in range(3):
        inner = blk_r[lvl] == blk_c[lvl]
        outer = (rowi // (16 << lvl)) == (coli // (16 << lvl))
        emask = outer & jnp.logical_not(inner)
        for it in items:
            e = jnp.where(emask, -xm[it], 0.0)
            m1 = e + _dot(nm[it].astype(BF16), e.astype(BF16))
            ded = m1 + _dot(m1.astype(BF16), nm[it].astype(BF16))
            nm[it] = nm[it] - ded

    w_c, u_c, attn, q_dec, k_dec, g_last = {}, {}, {}, {}, {}, {}
    for it in items:
        e_gc = jnp.exp(gcc[it])
        rhs = jnp.concatenate([kn[it] * (beta[it] * e_gc), vv[it] * beta[it]], axis=1)
        sol = rhs + _dot(nm[it].astype(BF16), rhs.astype(BF16))
        w_c[it] = sol[:, :HEAD_DIM]
        u_c[it] = sol[:, HEAD_DIM:]
        attn[it] = (qk[it] * gamma[it]).astype(BF16)
        gc_last = gcc[it][c - 1:c, :]
        q_dec[it] = qn[it] * e_gc
        k_dec[it] = (kn[it] * jnp.exp(gc_last - gcc[it])).astype(BF16)
        g_last[it] = jnp.exp(gc_last)

    for it in items:
        ci, hh = it
        sl = slice(ci * c, (ci + 1) * c)
        hs = slice(hh * HEAD_DIM, (hh + 1) * HEAD_DIM)
        st = state_ref[hh]
        ws = _dot(jnp.concatenate([w_c[it], q_dec[it]], axis=0).astype(BF16), st.astype(BF16))
        vnb = (u_c[it] - ws[:c]).astype(BF16)
        o = ws[c:] + _dot(attn[it], vnb)
        state_ref[hh] = st * g_last[it] + _dot_tn(k_dec[it], vnb)
        o = _rms_rows(o) * ng * _silu(z_ref[sl, hs].astype(F32))
        o_ref[sl, hs] = o.astype(o_ref.dtype)


def deltanet_mix(p, conv_w8, small, small_t, crow, ccol, norm_g, batch, seq, n_heads,
                 q_blk, k_blk, v_blk, z_blk, tr=256):
    t = p.shape[0]
    tr = min(tr, seq)
    ns = seq // tr
    ngroups = n_heads // DN_GROUP
    gw = DN_GROUP * HEAD_DIM
    kern = functools.partial(_dn_kernel, tr=tr)
    row = lambda b, g, s: b * ns + s
    return pl.pallas_call(
        kern,
        out_shape=jax.ShapeDtypeStruct((t, n_heads * HEAD_DIM), BF16),
        grid=(batch, ngroups, ns),
        in_specs=[pl.BlockSpec((tr, gw), lambda b, g, s: (row(b, g, s), q_blk + g)),
                  pl.BlockSpec((tr, gw), lambda b, g, s: (row(b, g, s), k_blk + g)),
                  pl.BlockSpec((tr, gw), lambda b, g, s: (row(b, g, s), v_blk + g)),
                  pl.BlockSpec((tr, gw), lambda b, g, s: (row(b, g, s), z_blk + g)),
                  pl.BlockSpec((8, gw), lambda b, g, s: (0, q_blk + g)),
                  pl.BlockSpec((8, gw), lambda b, g, s: (0, k_blk + g)),
                  pl.BlockSpec((8, gw), lambda b, g, s: (0, v_blk + g)),
                  pl.BlockSpec((tr, LANES), lambda b, g, s: (row(b, g, s), g)),
                  pl.BlockSpec((None, None, 8, tr), lambda b, g, s: (b, g, 0, s)),
                  pl.BlockSpec((None, 8, LANES), lambda b, g, s: (g, 0, 0)),
                  pl.BlockSpec((None, 2, 8, LANES), lambda b, g, s: (g, 0, 0, 0)),
                  pl.BlockSpec((1, HEAD_DIM), lambda b, g, s: (0, 0))],
        out_specs=pl.BlockSpec((tr, gw), lambda b, g, s: (row(b, g, s), g)),
        scratch_shapes=[pltpu.VMEM((DN_GROUP, HEAD_DIM, HEAD_DIM), F32),
                        pltpu.VMEM((tr + 8, gw), F32), pltpu.VMEM((tr + 8, gw), F32),
                        pltpu.VMEM((tr + 8, gw), F32),
                        pltpu.VMEM((tr, gw), F32), pltpu.VMEM((tr, gw), F32), pltpu.VMEM((tr, gw), F32)],
        compiler_params=_cparams(3), name="deltanet",
    )(p, p, p, p, conv_w8, conv_w8, conv_w8, small, small_t, crow, ccol, norm_g)


def _norm_router_kernel(x_ref, g_ref, sc_ref, sh_ref, wr_ref, br_ref, h_ref, lg_ref):
    x = x_ref[...]
    h = _rms_rows(x) * g_ref[...] * (1.0 + sc_ref[0]) + sh_ref[0]
    h_ref[...] = h
    w = wr_ref[...]
    w_hi = w.astype(BF16)
    w_lo = (w - w_hi.astype(F32)).astype(BF16)
    h_hi = h.astype(BF16)
    h_lo = (h - h_hi.astype(F32)).astype(BF16)
    lg_ref[...] = _dot(h_hi, w_hi) + _dot(h_lo, w_hi) + _dot(h_hi, w_lo) + br_ref[...]


def norm_router(x2d, g, sc, sh, w_router_pad, b_router_pad, seq, tm=256):
    t, d = x2d.shape
    tm = min(tm, seq)
    nb = seq // tm
    return pl.pallas_call(
        _norm_router_kernel,
        out_shape=(jax.ShapeDtypeStruct((t, d), F32), jax.ShapeDtypeStruct((t, LANES), F32)),
        grid=(t // tm,),
        in_specs=[pl.BlockSpec((tm, d), lambda i: (i, 0)),
                  pl.BlockSpec((1, d), lambda i: (0, 0)),
                  pl.BlockSpec((1, 1, d), lambda i: (i // nb, 0, 0)),
                  pl.BlockSpec((1, 1, d), lambda i: (i // nb, 0, 0)),
                  pl.BlockSpec((d, LANES), lambda i: (0, 0)),
                  pl.BlockSpec((1, LANES), lambda i: (0, 0))],
        out_specs=(pl.BlockSpec((tm, d), lambda i: (i, 0)),
                   pl.BlockSpec((tm, LANES), lambda i: (i, 0))),
        compiler_params=_cparams(1), name="norm_router",
    )(x2d, g.reshape(1, d), sc, sh, w_router_pad, b_router_pad)


def _moe_kernel(blk_e_ref, first_ref, nxt_ref, nused_ref, tok_ref, tokn_ref, h_hbm, wgu_hbm, wd_hbm,
                bgu_ref, bd_ref, rw_ref, y_ref,
                stg_gu, stg_d, wgu_b, wd_b, perm_ref, xbuf, wsem, gsem, *, tb, layer):
    i = pl.program_id(0)
    nused = nused_ref[0]
    slot = i % 2
    d, f2 = stg_gu.shape
    f = f2 // 2

    def weight_copies(e):
        return (pltpu.make_async_copy(wgu_hbm.at[layer, e], stg_gu, wsem.at[0]),
                pltpu.make_async_copy(wd_hbm.at[layer, e], stg_d, wsem.at[1]))

    def row_copy(toks, r, s):
        return pltpu.make_async_copy(h_hbm.at[pl.ds(toks[0, r], 1), :],
                                     xbuf.at[s, pl.ds(r, 1), :], gsem.at[s])

    def start_gather(toks, s):
        def issue(r, carry):
            row_copy(toks, r, s).start()
            return carry
        lax.fori_loop(0, tb, issue, 0)

    @pl.when(i == 0)
    def _():
        for cp in weight_copies(blk_e_ref[0]):
            cp.start()
        start_gather(tok_ref, 0)
        rr = lax.broadcasted_iota(jnp.int32, (f2, f), 0)
        cc = lax.broadcasted_iota(jnp.int32, (f2, f), 1)
        perm_ref[...] = jnp.where(rr == 2 * cc, 1.0, 0.0).astype(BF16)

    @pl.when((i < nused) & (first_ref[i] == 1))
    def _():
        for cp in weight_copies(0):
            cp.wait()
        rows = min(256, f)

        def cast_gu(r, carry):
            r0 = pl.multiple_of(r * rows, rows)
            wgu_b[pl.ds(r0, rows), :] = stg_gu[pl.ds(r0, rows), :].astype(BF16)
            return carry
        lax.fori_loop(0, d // rows, cast_gu, 0)

        def cast_d(r, carry):
            r0 = pl.multiple_of(r * rows, rows)
            wd_b[pl.ds(r0, rows), :] = stg_d[pl.ds(r0, rows), :].astype(BF16)
            return carry
        lax.fori_loop(0, f // rows, cast_d, 0)

        @pl.when(nxt_ref[i] >= 0)
        def _():
            for cp in weight_copies(nxt_ref[i]):
                cp.start()

    @pl.when(i + 1 < nused)
    def _():
        start_gather(tokn_ref, 1 - slot)

    @pl.when(i < nused)
    def _():
        def drain(r, carry):
            row_copy(tok_ref, r, slot).wait()
            return carry
        lax.fori_loop(0, tb, drain, 0)

        x = xbuf[slot].astype(BF16)
        gu = _dot(x, wgu_b[...]) + bgu_ref[...]
        gate = jnp.minimum(gu, SWIGLU_LIMIT)
        sg = gate * _sigmoid(SWIGLU_ALPHA * gate)
        lin1 = jnp.clip(gu, -SWIGLU_LIMIT, SWIGLU_LIMIT) + 1.0
        act_il = sg * pltpu.roll(lin1, f2 - 1, 1)
        act = _dot(act_il.astype(BF16), perm_ref[...])
        y = (_dot(act.astype(BF16), wd_b[...]) + bd_ref[...]) * rw_ref[...]
        y_ref[...] = _bf16_bits_lo(y[:, :d // 2]) | _bf16_bits_hi(y[:, d // 2:])

    @pl.when(i >= nused)
    def _():
        y_ref[...] = jnp.zeros_like(y_ref)


def moe_experts(h2, w_gu, w_down, b_gu_l, b_down_l, layer, blk_e, first, nxt, n_used, row_tok, row_w, tb):
    t, d = h2.shape
    n_blk = row_tok.shape[0]
    f2 = w_gu.shape[3]
    f = f2 // 2
    kern = functools.partial(_moe_kernel, tb=tb, layer=layer)
    grid_spec = pltpu.PrefetchScalarGridSpec(
        num_scalar_prefetch=4,
        grid=(n_blk,),
        in_specs=[pl.BlockSpec((None, 1, tb), lambda i, *_: (i, 0, 0), memory_space=pltpu.SMEM),
                  pl.BlockSpec((None, 1, tb), lambda i, *_: (jnp.minimum(i + 1, n_blk - 1), 0, 0),
                               memory_space=pltpu.SMEM),
                  pl.BlockSpec(memory_space=pl.ANY),
                  pl.BlockSpec(memory_space=pl.ANY),
                  pl.BlockSpec(memory_space=pl.ANY),
                  pl.BlockSpec((None, 1, f2), lambda i, be, *_: (be[i], 0, 0)),
                  pl.BlockSpec((None, 1, d), lambda i, be, *_: (be[i], 0, 0)),
                  pl.BlockSpec((tb, 1), lambda i, *_: (i, 0))],
        out_specs=pl.BlockSpec((tb, d // 2), lambda i, *_: (i, 0)),
        scratch_shapes=[pltpu.VMEM((d, f2), F32), pltpu.VMEM((f, d), F32),
                        pltpu.VMEM((d, f2), BF16), pltpu.VMEM((f, d), BF16),
                        pltpu.VMEM((f2, f), BF16), pltpu.VMEM((2, tb, d), F32),
                        pltpu.SemaphoreType.DMA((2,)), pltpu.SemaphoreType.DMA((2,))])
    return pl.pallas_call(
        kern,
        out_shape=jax.ShapeDtypeStruct((n_blk * tb, d // 2), U32),
        grid_spec=grid_spec,
        compiler_params=_cparams(1), name="moe_experts",
    )(blk_e, first, nxt, n_used, row_tok, row_tok, h2, w_gu, w_down, b_gu_l, b_down_l, row_w)


def _combine_kernel(pos_ref, y_hbm, x_ref, g2_ref, ng_ref, sc_ref, sh_ref, *rest, tm, final):
    if final:
        out_ref, ybuf, sem = rest
    else:
        xo_ref, h_ref, ybuf, sem = rest

    def row_copy(r, k):
        return pltpu.make_async_copy(y_hbm.at[pl.ds(pos_ref[0, r * TOP_K + k], 1), :],
                                     ybuf.at[k, pl.ds(r, 1), :], sem.at[0])

    def issue(r, carry):
        for k in range(TOP_K):
            row_copy(r, k).start()
        return carry
    lax.fori_loop(0, tm, issue, 0)

    def drain(r, carry):
        for k in range(TOP_K):
            row_copy(r, k).wait()
        return carry
    lax.fori_loop(0, tm, drain, 0)

    def lo(w):
        return pltpu.bitcast(w << 16, F32)

    def hi(w):
        return pltpu.bitcast(w & jnp.uint32(HI16), F32)

    w0, w1, w2, w3 = ybuf[0], ybuf[1], ybuf[2], ybuf[3]
    moe = jnp.concatenate([(lo(w0) + lo(w1)) + (lo(w2) + lo(w3)),
                           (hi(w0) + hi(w1)) + (hi(w2) + hi(w3))], axis=1)
    x = x_ref[...] + g2_ref[0] * moe
    if final:
        out_ref[...] = _rms_rows(x) * ng_ref[...]
    else:
        xo_ref[...] = x
        h = _rms_rows(x) * ng_ref[...] * (1.0 + sc_ref[0]) + sh_ref[0]
        h_ref[...] = h.astype(h_ref.dtype)


def moe_combine(y, pos, x2d, gate2, norm_g, sc, sh, seq, final, tm=128):
    t, d = x2d.shape
    tm = min(tm, seq)
    nb = seq // tm
    kern = functools.partial(_combine_kernel, tm=tm, final=final)
    if final:
        out_shape = jax.ShapeDtypeStruct((t, d), F32)
        out_specs = pl.BlockSpec((tm, d), lambda i: (i, 0))
    else:
        out_shape = (jax.ShapeDtypeStruct((t, d), F32), jax.ShapeDtypeStruct((t, d), BF16))
        out_specs = (pl.BlockSpec((tm, d), lambda i: (i, 0)), pl.BlockSpec((tm, d), lambda i: (i, 0)))
    return pl.pallas_call(
        kern,
        out_shape=out_shape,
        grid=(t // tm,),
        in_specs=[pl.BlockSpec((None, 1, tm * TOP_K), lambda i: (i, 0, 0), memory_space=pltpu.SMEM),
                  pl.BlockSpec(memory_space=pl.ANY),
                  pl.BlockSpec((tm, d), lambda i: (i, 0)),
                  pl.BlockSpec((1, 1, d), lambda i: (i // nb, 0, 0)),
                  pl.BlockSpec((1, d), lambda i: (0, 0)),
                  pl.BlockSpec((1, 1, d), lambda i: (i // nb, 0, 0)),
                  pl.BlockSpec((1, 1, d), lambda i: (i // nb, 0, 0))],
        out_specs=out_specs,
        scratch_shapes=[pltpu.VMEM((TOP_K, tm, d // 2), U32), pltpu.SemaphoreType.DMA((1,))],
        compiler_params=_cparams(1), name="moe_combine",
    )(pos.reshape(t // tm, 1, tm * TOP_K), y, x2d, gate2, norm_g.reshape(1, d), sc, sh)


def _routing_tables(logits, n_experts, tb):
    t = logits.shape[0]
    top_v, top_e = lax.top_k(logits[:, :n_experts], TOP_K)
    top_w = jax.nn.softmax(top_v, axis=-1)
    n_a = t * TOP_K
    e_a = top_e.reshape(n_a)
    eids = jnp.arange(n_experts, dtype=jnp.int32)
    onehot = (e_a[:, None] == eids[None, :]).astype(jnp.int32)
    csum = jnp.cumsum(onehot, axis=0)
    counts = csum[-1]
    rank = jnp.sum(jnp.where(onehot > 0, csum, 0), axis=1) - 1
    padded = (counts + tb - 1) // tb * tb
    pend = jnp.cumsum(padded)
    pstart = pend - padded
    pos = (pstart[e_a] + rank).astype(jnp.int32)
    n_rows = (n_a + n_experts * (tb - 1) + tb - 1) // tb * tb
    n_blk = n_rows // tb
    t_a = jnp.repeat(jnp.arange(t, dtype=jnp.int32), TOP_K)
    row_tok = jnp.zeros((n_rows,), jnp.int32).at[pos].set(t_a)
    row_w = jnp.zeros((n_rows,), F32).at[pos].set(top_w.reshape(n_a))
    blk_start = jnp.arange(n_blk, dtype=jnp.int32) * tb
    blk_e = jnp.minimum(jnp.searchsorted(pend, blk_start, side='right'), n_experts - 1).astype(jnp.int32)
    n_used = (pend[-1] // tb).astype(jnp.int32)
    used = jnp.arange(n_blk) < n_used
    first = (used & (blk_start == pstart[blk_e])).astype(jnp.int32)
    has = counts > 0
    cand = jnp.where(has, eids, n_experts)
    suffix_min = lax.cummin(cand, axis=0, reverse=True)
    nxt_e = jnp.concatenate([suffix_min[1:], jnp.full((1,), n_experts, jnp.int32)])
    nxt_e = jnp.where(nxt_e >= n_experts, -1, nxt_e).astype(jnp.int32)
    nxt = nxt_e[blk_e]
    return (pos, row_tok.reshape(n_blk, 1, tb), row_w.reshape(n_rows, 1), blk_e, first, nxt,
            n_used.reshape(1))


def _rope_tables(seq):
    half = HEAD_DIM // 2
    inv = ROPE_BASE ** (-jnp.arange(half, dtype=F32) / half)
    ang = jnp.arange(seq, dtype=F32)[:, None] * inv[None, :]
    cos, sin = jnp.cos(ang), jnp.sin(ang)
    return jnp.concatenate([cos, cos], axis=-1), jnp.concatenate([-sin, sin], axis=-1)


def kernel(x, c, norm1_g, w_ada, b_ada, w_in, dn_conv_w, dn_a_log, dn_dt_bias, dn_norm_g, fox_f_bias,
           w_out, norm2_g, w_router, b_router, w_gu, b_gu, w_down, b_down, final_norm_g):
    batch, seq, d = x.shape
    depth = w_ada.shape[0]
    t = batch * seq
    dn_heads = dn_a_log.shape[1]
    fox_heads = fox_f_bias.shape[1]
    ret_heads = d // HEAD_DIM - dn_heads - fox_heads
    dn_w, ret_w, fox_w = dn_heads * HEAD_DIM, ret_heads * HEAD_DIM, fox_heads * HEAD_DIM
    n_experts = w_router.shape[2]
    n_groups = dn_heads // DN_GROUP
    gw = DN_GROUP * HEAD_DIM
    moe_tb = 256
    tn = 512

    widths = (3 * dn_w, dn_w, dn_heads, dn_heads, ret_w, ret_w, ret_w, ret_w,
              fox_w, fox_w, fox_w, fox_w, fox_heads)
    cuts = np.concatenate([[0], np.cumsum(widths)])
    o_b, o_a, o_r, o_ff = int(cuts[2]), int(cuts[3]), int(cuts[4]), int(cuts[12])
    n_a_tiles = o_b // tn
    n_bc_tiles = (o_ff - o_r) // tn
    small_src, small_dst = [], []
    for g in range(n_groups):
        for hh in range(DN_GROUP):
            small_src += [o_b + g * DN_GROUP + hh, o_a + g * DN_GROUP + hh]
            small_dst += [g * LANES + hh, g * LANES + DN_GROUP + hh]
    for hh in range(fox_heads):
        small_src.append(o_ff + hh)
        small_dst.append(n_groups * LANES + hh)
    small_src = np.asarray(small_src, np.int32)
    small_dst = np.asarray(small_dst, np.int32)

    x2d = x.reshape(t, d)
    c_pad = jnp.zeros((8, d), BF16).at[:batch].set(c.astype(BF16))
    mod = ada_mod(c_pad, w_ada, b_ada)[:, :batch, :]
    mod = mod.reshape(depth, batch, 6, 1, d)
    cos2, sin2 = _rope_tables(seq)
    lg_tab = jnp.log(1.0 - 2.0 ** (-5.0 - jnp.arange(ret_heads, dtype=F32)))
    lg_tab = jnp.broadcast_to(lg_tab[:, None, None], (ret_heads, 8, LANES))

    dn_q_blk, dn_k_blk, dn_v_blk, dn_z_blk = 0, dn_w // gw, 2 * dn_w // gw, 3 * dn_w // gw
    r_q_blk, r_k_blk, r_v_blk, r_g_blk = 0, ret_heads, 2 * ret_heads, 3 * ret_heads
    f0 = 4 * ret_heads
    f_q_blk, f_k_blk, f_v_blk, f_z_blk = f0, f0 + fox_heads, f0 + 2 * fox_heads, f0 + 3 * fox_heads

    h = norm_mod(x2d, norm1_g[0], mod[0, :, 1], mod[0, :, 0], seq)
    out = None
    for l in range(depth):
        sh1, sc1, g1, sh2, sc2, g2 = (mod[l, :, i] for i in range(6))
        p_a = in_proj(h, w_in, l, 0, n_a_tiles - 1, w_in[l][:, o_b - tn:o_b].astype(BF16))
        p_bc = in_proj(h, w_in, l, o_r, n_bc_tiles - 1, w_in[l][:, o_ff - tn:o_ff].astype(BF16))
        w_small = jnp.zeros((d, (n_groups + 1) * LANES), BF16).at[:, small_dst].set(
            w_in[l][:, small_src].astype(BF16))
        small = matmul_bf16(h, w_small, F32)

        conv_w8 = jnp.zeros((8, 3 * dn_w), F32).at[:CONV_K].set(dn_conv_w[l])
        small_t = small[:, :n_groups * LANES].reshape(batch, seq, n_groups, LANES)[..., :8]
        small_t = small_t.transpose(0, 2, 3, 1)
        dtb = dn_dt_bias[l].reshape(n_groups, DN_GROUP)
        nega = -jnp.exp(dn_a_log[l]).reshape(n_groups, DN_GROUP)
        zg = jnp.zeros((n_groups, DN_GROUP), F32)
        pad = jnp.zeros((n_groups, LANES - 2 * DN_GROUP), F32)
        crow = jnp.stack([jnp.concatenate([zg, dtb, pad], axis=1),
                          jnp.concatenate([zg, nega, pad], axis=1)], axis=1)
        crow = jnp.concatenate([crow, jnp.zeros((n_groups, 6, LANES), F32)], axis=1)
        ccol = jnp.stack([jnp.concatenate([zg, dtb], axis=1), jnp.concatenate([zg, nega], axis=1)], axis=1)
        ccol = jnp.broadcast_to(ccol[..., None], (n_groups, 2, 8, LANES))
        o_dn = deltanet_mix(p_a, conv_w8, small, small_t, crow, ccol, dn_norm_g[l].reshape(1, HEAD_DIM),
                            batch, seq, dn_heads, dn_q_blk, dn_k_blk, dn_v_blk, dn_z_blk)
        o_ret = retention_mix(p_bc, cos2, sin2, lg_tab, batch, seq, ret_heads,
                              r_q_blk, r_k_blk, r_v_blk, r_g_blk)
        fb = jnp.zeros((1, LANES), F32).at[0, :fox_heads].set(fox_f_bias[l])
        f_cum = fox_prefix(small, fb, batch, seq, n_groups)
        o_fox = fox_attention(p_bc, f_cum, batch, seq, fox_heads, f_q_blk, f_k_blk, f_v_blk, f_z_blk)
        x2d = out_proj_resid(o_dn, o_ret, o_fox, w_out, l, x2d, g1, seq)

        wr_pad = jnp.zeros((d, LANES), F32).at[:, :n_experts].set(w_router[l])
        br_pad = jnp.zeros((1, LANES), F32).at[0, :n_experts].set(b_router[l])
        h2, logits = norm_router(x2d, norm2_g[l], sc2, sh2, wr_pad, br_pad, seq)
        pos, row_tok, row_w, blk_e, first, nxt, n_used = _routing_tables(logits, n_experts, moe_tb)
        y = moe_experts(h2, w_gu, w_down, b_gu[l][:, None, :], b_down[l][:, None, :], l,
                        blk_e, first, nxt, n_used, row_tok, row_w, moe_tb)
        if l + 1 < depth:
            nsh1, nsc1 = mod[l + 1, :, 0], mod[l + 1, :, 1]
            x2d, h = moe_combine(y, pos, x2d, g2, norm1_g[l + 1], nsc1, nsh1, seq, final=False)
        else:
            out = moe_combine(y, pos, x2d, g2, final_norm_g, sc2, sh2, seq, final=True)
    return out.reshape(batch, seq, d)
```

```python
import functools

import numpy as np
import jax
import jax.numpy as jnp
from jax import lax
from jax.experimental import pallas as pl
from jax.experimental.pallas import tpu as pltpu

F32 = jnp.float32
BF16 = jnp.bfloat16
U32 = jnp.uint32

HEAD_DIM = 128
CONV_K = 4
DN_CHUNK = 64
RET_CHUNK = 128
ROPE_BASE = 10000.0
TOP_K = 4
SWIGLU_LIMIT = 7.0
SWIGLU_ALPHA = 1.702
EPS = 1e-6
DN_GROUP = 4
LANES = 128
VMEM_LIMIT = 56 * 1024 * 1024
LOG2E = 1.4426950408889634
HI16 = 0xFFFF0000


def _cparams(n_axes):
    return pltpu.CompilerParams(dimension_semantics=("arbitrary",) * n_axes,
                                vmem_limit_bytes=VMEM_LIMIT)


def _dot(a, b):
    return jnp.dot(a, b, preferred_element_type=F32)


def _dot_nt(a, b):
    return lax.dot_general(a, b, (((1,), (1,)), ((), ())), preferred_element_type=F32)


def _dot_tn(a, b):
    return lax.dot_general(a, b, (((0,), (0,)), ((), ())), preferred_element_type=F32)


def _split3(x):
    h = x.astype(BF16)
    r = x - h.astype(F32)
    m = r.astype(BF16)
    l = (r - m.astype(F32)).astype(BF16)
    return h, m, l


def _sigmoid(x):
    return 1.0 / (1.0 + jnp.exp(-x))


def _silu(x):
    return x * _sigmoid(x)


def _softplus(x):
    return jnp.maximum(x, 0.0) + jnp.log1p(jnp.exp(-jnp.abs(x)))


def _log_sigmoid(x):
    return jnp.minimum(x, 0.0) - jnp.log1p(jnp.exp(-jnp.abs(x)))


def _rms_rows(x):
    return x * lax.rsqrt(jnp.mean(x * x, axis=-1, keepdims=True) + EPS)


def _bf16_bits_lo(x):
    return pltpu.bitcast(x.astype(BF16).astype(F32), U32) >> 16


def _bf16_bits_hi(x):
    return pltpu.bitcast(x.astype(BF16).astype(F32), U32) & jnp.uint32(HI16)


def _ada_kernel(c_ref, w_ref, b_ref, o_ref):
    o_ref[...] = _dot(c_ref[...], w_ref[...].astype(BF16)) + b_ref[...]


def ada_mod(c_pad, w_ada, b_ada, tn=512):
    depth, d, n = w_ada.shape
    return pl.pallas_call(
        _ada_kernel,
        out_shape=jax.ShapeDtypeStruct((depth, 8, n), F32),
        grid=(depth, n // tn),
        in_specs=[pl.BlockSpec((8, d), lambda l, j: (0, 0)),
                  pl.BlockSpec((None, d, tn), lambda l, j: (l, 0, j)),
                  pl.BlockSpec((None, 1, tn), lambda l, j: (l, 0, j))],
        out_specs=pl.BlockSpec((None, 8, tn), lambda l, j: (l, 0, j)),
        compiler_params=_cparams(2), name="ada_mod",
    )(c_pad, w_ada, b_ada.reshape(depth, 1, n))


def _norm_mod_kernel(x_ref, g_ref, sc_ref, sh_ref, h_ref):
    x = x_ref[...]
    h = _rms_rows(x) * g_ref[...] * (1.0 + sc_ref[0]) + sh_ref[0]
    h_ref[...] = h.astype(h_ref.dtype)


def norm_mod(x2d, g, sc, sh, seq, tm=256):
    t, d = x2d.shape
    tm = min(tm, seq)
    nb = seq // tm
    return pl.pallas_call(
        _norm_mod_kernel,
        out_shape=jax.ShapeDtypeStruct((t, d), BF16),
        grid=(t // tm,),
        in_specs=[pl.BlockSpec((tm, d), lambda i: (i, 0)),
                  pl.BlockSpec((1, d), lambda i: (0, 0)),
                  pl.BlockSpec((1, 1, d), lambda i: (i // nb, 0, 0)),
                  pl.BlockSpec((1, 1, d), lambda i: (i // nb, 0, 0))],
        out_specs=pl.BlockSpec((tm, d), lambda i: (i, 0)),
        compiler_params=_cparams(1), name="norm_mod",
    )(x2d, g.reshape(1, d), sc, sh)


def _w_prep_kernel(w_ref, o_ref):
    for l in range(o_ref.shape[0]):
        o_ref[l] = w_ref[:, l, :].astype(BF16)


def w_in_prep(w_in_t, gap_start, gap_len, n_rows, rows=256):
    n, depth, k = w_in_t.shape
    rows = min(rows, gap_start)
    assert gap_start % rows == 0 and n_rows % rows == 0 and gap_len % 8 == 0

    def in_idx(j):
        r = j * rows
        return (r + jnp.where(r >= gap_start, gap_len, 0), 0, 0)
    return pl.pallas_call(
        _w_prep_kernel,
        out_shape=jax.ShapeDtypeStruct((depth, n_rows, k), BF16),
        grid=(n_rows // rows,),
        in_specs=[pl.BlockSpec((pl.Element(rows), pl.Element(depth), pl.Element(k)), in_idx)],
        out_specs=pl.BlockSpec((depth, rows, k), lambda j: (0, j, 0)),
        compiler_params=_cparams(1), name="w_in_prep",
    )(w_in_t)


def _mm_nt_kernel(x_ref, w_ref, o_ref):
    o_ref[...] = _dot_nt(x_ref[...], w_ref[...]).astype(o_ref.dtype)


def matmul_nt(x, w_t, layer, out_dtype, name, tm=1024, tn=512):
    m, k = x.shape
    n = w_t.shape[1]
    tm = min(tm, m)
    tn = min(tn, n)
    return pl.pallas_call(
        _mm_nt_kernel,
        out_shape=jax.ShapeDtypeStruct((m, n), out_dtype),
        grid=(n // tn, m // tm),
        in_specs=[pl.BlockSpec((tm, k), lambda j, i: (i, 0)),
                  pl.BlockSpec((None, tn, k), lambda j, i: (layer, j, 0))],
        out_specs=pl.BlockSpec((tm, tn), lambda j, i: (i, j)),
        compiler_params=_cparams(2), name=name,
    )(x, w_t)


def _mm_resid_kernel(a_ref, b_ref, c_ref, w_ref, x_ref, g_ref, o_ref, wb_ref, *, splits):
    @pl.when(pl.program_id(1) == 0)
    def _():
        wb_ref[...] = w_ref[...].astype(BF16)
    k0, k1 = splits
    y = (_dot(a_ref[...], wb_ref[0:k0, :]) + _dot(b_ref[...], wb_ref[k0:k1, :])
         + _dot(c_ref[...], wb_ref[k1:, :]))
    o_ref[...] = x_ref[...] + g_ref[0] * y


def out_proj_resid(a, b, c, w_out, layer, x2d, gate, seq, tm=1024, tn=512):
    m = a.shape[0]
    ka, kb, kc = a.shape[1], b.shape[1], c.shape[1]
    k = ka + kb + kc
    n = w_out.shape[2]
    tm = min(tm, seq)
    nb = seq // tm
    kern = functools.partial(_mm_resid_kernel, splits=(ka, ka + kb))
    return pl.pallas_call(
        kern,
        out_shape=jax.ShapeDtypeStruct((m, n), F32),
        grid=(n // tn, m // tm),
        in_specs=[pl.BlockSpec((tm, ka), lambda j, i: (i, 0)),
                  pl.BlockSpec((tm, kb), lambda j, i: (i, 0)),
                  pl.BlockSpec((tm, kc), lambda j, i: (i, 0)),
                  pl.BlockSpec((None, k, tn), lambda j, i: (layer, 0, j)),
                  pl.BlockSpec((tm, tn), lambda j, i: (i, j)),
                  pl.BlockSpec((1, 1, tn), lambda j, i: (i // nb, 0, j))],
        out_specs=pl.BlockSpec((tm, tn), lambda j, i: (i, j)),
        scratch_shapes=[pltpu.VMEM((k, tn), BF16)],
        compiler_params=_cparams(2), name="out_proj",
    )(a, b, c, w_out, x2d, gate)


def _fprep_kernel(s_ref, b_ref, f_ref, carry_ref):
    @pl.when(pl.program_id(1) == 0)
    def _():
        carry_ref[...] = jnp.zeros_like(carry_ref)
    tq = s_ref.shape[0]
    lf = _log_sigmoid(s_ref[...] + b_ref[...])
    row = lax.broadcasted_iota(jnp.int32, (tq, tq), 0)
    col = lax.broadcasted_iota(jnp.int32, (tq, tq), 1)
    tri = jnp.where(row >= col, 1.0, 0.0).astype(BF16)
    h, m, l = _split3(lf)
    f = _dot(tri, h) + _dot(tri, m) + _dot(tri, l) + carry_ref[0:1, :]
    f_ref[...] = f
    carry_ref[...] = jnp.broadcast_to(f[tq - 1:tq, :], carry_ref.shape)


def fox_prefix(small, bias_row, batch, seq, col_block, tq=512):
    t = small.shape[0]
    tq = min(tq, seq)
    nq = seq // tq
    return pl.pallas_call(
        _fprep_kernel,
        out_shape=jax.ShapeDtypeStruct((t, LANES), F32),
        grid=(batch, nq),
        in_specs=[pl.BlockSpec((tq, LANES), lambda b, i: (b * nq + i, col_block)),
                  pl.BlockSpec((1, LANES), lambda b, i: (0, 0))],
        out_specs=pl.BlockSpec((tq, LANES), lambda b, i: (b * nq + i, 0)),
        scratch_shapes=[pltpu.VMEM((8, LANES), F32)],
        compiler_params=_cparams(2), name="fox_prefix",
    )(small, bias_row)


def _fox_kernel(q_ref, k_ref, v_ref, z_ref, f_ref, o_ref, ka_ref, va_ref, m_ref, acc_ref, *, tq, seq):
    h = pl.program_id(1)
    qi = pl.program_id(2)
    lane = lax.broadcasted_iota(jnp.int32, (tq, LANES), 1)

    def f_column(rows):
        fb = f_ref[pl.ds(rows, tq), :]
        return jnp.sum(jnp.where(lane == h, fb, 0.0), axis=-1, keepdims=True) * LOG2E

    def aug_lanes(fcol, sign_first):
        fh, fm, fl = _split3(fcol)
        fh, fm, fl = fh.astype(F32), fm.astype(F32), fl.astype(F32)
        one = jnp.ones((tq, LANES), F32)
        zero = jnp.zeros((tq, LANES), F32)
        if sign_first:
            a = jnp.where(lane == 0, fh, jnp.where(lane == 1, fm, jnp.where(lane == 2, fl,
                          jnp.where(lane < 6, one, zero))))
        else:
            a = jnp.where(lane < 3, one, jnp.where(lane == 3, -fh, jnp.where(lane == 4, -fm,
                          jnp.where(lane == 5, -fl, zero))))
        return a.astype(BF16)

    @pl.when(qi == 0)
    def _():
        def build(j, carry):
            r0 = pl.multiple_of(j * tq, tq)
            ka_ref[pl.ds(r0, tq), 0:HEAD_DIM] = k_ref[pl.ds(r0, tq), :]
            ka_ref[pl.ds(r0, tq), HEAD_DIM:] = aug_lanes(f_column(r0), False)
            va_ref[pl.ds(r0, tq), 0:HEAD_DIM] = v_ref[pl.ds(r0, tq), :]
            va_ref[pl.ds(r0, tq), HEAD_DIM:] = jnp.ones((tq, HEAD_DIM), BF16)
            return carry
        lax.fori_loop(0, seq // tq, build, 0)

    q0 = pl.multiple_of(qi * tq, tq)
    qs = (q_ref[...].astype(F32) * (HEAD_DIM ** -0.5 * LOG2E)).astype(BF16)
    qa = jnp.concatenate([qs, aug_lanes(f_column(q0), True)], axis=1)

    m_ref[...] = jnp.full_like(m_ref, -1e30)
    acc_ref[...] = jnp.zeros_like(acc_ref)

    def block(k0, width, masked):
        s = _dot_nt(qa, ka_ref[pl.ds(k0, width), :])
        if masked:
            row = lax.broadcasted_iota(jnp.int32, (tq, width), 0)
            col = lax.broadcasted_iota(jnp.int32, (tq, width), 1)
            s = jnp.where(row >= col, s, -1e30)
        m_old = m_ref[...]
        m_new = jnp.maximum(m_old, jnp.max(s, axis=-1, keepdims=True))
        p = jnp.exp2(s - m_new).astype(BF16)
        acc_ref[...] = jnp.exp2(m_old - m_new) * acc_ref[...] + _dot(p, va_ref[pl.ds(k0, width), :])
        m_ref[...] = m_new

    def wide(j, carry):
        block(pl.multiple_of(j * (2 * tq), 2 * tq), 2 * tq, False)
        return carry
    lax.fori_loop(0, qi // 2, wide, 0)

    @pl.when(qi % 2 == 1)
    def _():
        block(pl.multiple_of((qi - 1) * tq, tq), tq, False)

    block(q0, tq, True)
    acc = acc_ref[...]
    o = acc[:, :HEAD_DIM] / acc[:, HEAD_DIM:]
    o_ref[...] = (o * _sigmoid(z_ref[...].astype(F32))).astype(o_ref.dtype)


def fox_attention(p, f_cum, batch, seq, n_heads, q_blk, k_blk, v_blk, z_blk, tq=512):
    t = p.shape[0]
    tq = min(tq, seq)
    nq = seq // tq
    kern = functools.partial(_fox_kernel, tq=tq, seq=seq)
    return pl.pallas_call(
        kern,
        out_shape=jax.ShapeDtypeStruct((t, n_heads * HEAD_DIM), BF16),
        grid=(batch, n_heads, nq),
        in_specs=[pl.BlockSpec((tq, HEAD_DIM), lambda b, h, i: (b * nq + i, q_blk + h)),
                  pl.BlockSpec((seq, HEAD_DIM), lambda b, h, i: (b, k_blk + h)),
                  pl.BlockSpec((seq, HEAD_DIM), lambda b, h, i: (b, v_blk + h)),
                  pl.BlockSpec((tq, HEAD_DIM), lambda b, h, i: (b * nq + i, z_blk + h)),
                  pl.BlockSpec((seq, LANES), lambda b, h, i: (b, 0))],
        out_specs=pl.BlockSpec((tq, HEAD_DIM), lambda b, h, i: (b * nq + i, h)),
        scratch_shapes=[pltpu.VMEM((seq, 2 * HEAD_DIM), BF16), pltpu.VMEM((seq, 2 * HEAD_DIM), BF16),
                        pltpu.VMEM((tq, 1), F32), pltpu.VMEM((tq, 2 * HEAD_DIM), F32)],
        compiler_params=_cparams(3), name="fox_attn",
    )(p, p, p, p, f_cum)


def _ret_kernel(q_ref, k_ref, v_ref, g_ref, cos_ref, sin_ref, lg_ref, o_ref, r_ref, *, tc):
    c = RET_CHUNK

    @pl.when(pl.program_id(2) == 0)
    def _():
        r_ref[...] = jnp.zeros_like(r_ref)

    lg = lg_ref[0:1, 0:1]
    rowi = lax.broadcasted_iota(jnp.int32, (c, c), 0)
    coli = lax.broadcasted_iota(jnp.int32, (c, c), 1)
    diff = (rowi - coli).astype(F32)
    decay = jnp.where(diff >= 0, jnp.exp(lg * jnp.maximum(diff, 0.0)), 0.0)
    pos = lax.broadcasted_iota(jnp.int32, (c, 1), 0).astype(F32)
    q_scale = jnp.exp(lg * (pos + 1.0))
    k_scale = jnp.exp(lg * (c - 1.0 - pos))
    chunk_decay = jnp.exp(lg * c)

    def rope(t, cos2, sin2):
        return t * cos2 + pltpu.roll(t, HEAD_DIM // 2, 1) * sin2

    for ci in range(tc // c):
        sl = slice(ci * c, (ci + 1) * c)
        cos2 = cos_ref[sl, :]
        sin2 = sin_ref[sl, :]
        q = rope(q_ref[sl, :].astype(F32), cos2, sin2)
        k = rope(k_ref[sl, :].astype(F32), cos2, sin2) * (HEAD_DIM ** -0.5)
        v = v_ref[sl, :]
        inner = _dot_nt(q.astype(BF16), k.astype(BF16)) * decay
        r_prev = r_ref[...]
        o = _dot(inner.astype(BF16), v) + _dot((q * q_scale).astype(BF16), r_prev.astype(BF16))
        r_ref[...] = r_prev * chunk_decay + _dot_tn((k * k_scale).astype(BF16), v)
        o = _rms_rows(o) * _silu(g_ref[sl, :].astype(F32))
        o_ref[sl, :] = o.astype(o_ref.dtype)


def retention_mix(p, cos2, sin2, lg_tab, batch, seq, n_heads, q_blk, k_blk, v_blk, g_blk, tc=512):
    t = p.shape[0]
    tc = min(tc, seq)
    ns = seq // tc
    kern = functools.partial(_ret_kernel, tc=tc)
    return pl.pallas_call(
        kern,
        out_shape=jax.ShapeDtypeStruct((t, n_heads * HEAD_DIM), BF16),
        grid=(batch, n_heads, ns),
        in_specs=[pl.BlockSpec((tc, HEAD_DIM), lambda b, h, s: (b * ns + s, q_blk + h)),
                  pl.BlockSpec((tc, HEAD_DIM), lambda b, h, s: (b * ns + s, k_blk + h)),
                  pl.BlockSpec((tc, HEAD_DIM), lambda b, h, s: (b * ns + s, v_blk + h)),
                  pl.BlockSpec((tc, HEAD_DIM), lambda b, h, s: (b * ns + s, g_blk + h)),
                  pl.BlockSpec((tc, HEAD_DIM), lambda b, h, s: (s, 0)),
                  pl.BlockSpec((tc, HEAD_DIM), lambda b, h, s: (s, 0)),
                  pl.BlockSpec((None, 8, LANES), lambda b, h, s: (h, 0, 0))],
        out_specs=pl.BlockSpec((tc, HEAD_DIM), lambda b, h, s: (b * ns + s, h)),
        scratch_shapes=[pltpu.VMEM((HEAD_DIM, HEAD_DIM), F32)],
        compiler_params=_cparams(3), name="retention",
    )(p, p, p, p, cos2, sin2, lg_tab)


def _dn_kernel(uq_ref, uk_ref, uv_ref, z_ref, cwq_ref, cwk_ref, cwv_ref, sm_ref, smt_ref,
               crow_ref, ccol_ref, ng_ref, o_ref,
               state_ref, bq_ref, bk_ref, bv_ref, qs_ref, ks_ref, vs_ref, *, tr):
    c = DN_CHUNK
    g_heads = DN_GROUP
    gw = g_heads * HEAD_DIM

    @pl.when(pl.program_id(2) == 0)
    def _():
        state_ref[...] = jnp.zeros_like(state_ref)
        bq_ref[0:8, :] = jnp.zeros((8, gw), F32)
        bk_ref[0:8, :] = jnp.zeros((8, gw), F32)
        bv_ref[0:8, :] = jnp.zeros((8, gw), F32)

    def conv(u_ref, cw_ref, buf_ref, dst_ref):
        buf_ref[8:8 + tr, :] = u_ref[...].astype(F32)
        y = cw_ref[3:4, :] * buf_ref[8:8 + tr, :]
        for j in range(CONV_K - 1):
            y = y + cw_ref[j:j + 1, :] * buf_ref[5 + j:5 + j + tr, :]
        buf_ref[0:8, :] = buf_ref[tr:tr + 8, :]
        dst_ref[...] = _silu(y)

    conv(uq_ref, cwq_ref, bq_ref, qs_ref)
    conv(uk_ref, cwk_ref, bk_ref, ks_ref)
    conv(uv_ref, cwv_ref, bv_ref, vs_ref)

    sm = sm_ref[...]
    beta_all = _sigmoid(sm)
    g_all = crow_ref[1:2, :] * _softplus(sm + crow_ref[0:1, :])
    smt = smt_ref[...]
    g_rows = ccol_ref[1, :, 0:1] * _softplus(smt + ccol_ref[0, :, 0:1])

    rowi = lax.broadcasted_iota(jnp.int32, (c, c), 0)
    coli = lax.broadcasted_iota(jnp.int32, (c, c), 1)
    incl = rowi >= coli
    strict = rowi > coli
    tri_l = jnp.where(incl, 1.0, 0.0).astype(BF16)
    tri_u = jnp.where(rowi <= coli, 1.0, 0.0).astype(BF16)
    ng = ng_ref[...]
    n_chunks = tr // c
    items = [(ci, hh) for ci in range(n_chunks) for hh in range(g_heads)]

    gc_cols, gc_rowsl = [], []
    for ci in range(n_chunks):
        sl = slice(ci * c, (ci + 1) * c)
        gh, gm, gl = _split3(g_all[sl, :])
        gc_cols.append(_dot(tri_l, gh) + _dot(tri_l, gm) + _dot(tri_l, gl))
        rh, rm, rl = _split3(g_rows[:, sl])
        gc_rowsl.append(_dot(rh, tri_u) + _dot(rm, tri_u) + _dot(rl, tri_u))

    qn, kn, vv, beta, gcc, gamma, qk, xm = {}, {}, {}, {}, {}, {}, {}, {}
    for it in items:
        ci, hh = it
        sl = slice(ci * c, (ci + 1) * c)
        hs = slice(hh * HEAD_DIM, (hh + 1) * HEAD_DIM)
        q = qs_ref[sl, hs]
        k = ks_ref[sl, hs]
        vv[it] = vs_ref[sl, hs]
        q = q * lax.rsqrt(jnp.sum(q * q, axis=-1, keepdims=True) + EPS) * (HEAD_DIM ** -0.5)
        k = k * lax.rsqrt(jnp.sum(k * k, axis=-1, keepdims=True) + EPS)
        qn[it], kn[it] = q, k
        beta[it] = beta_all[sl, hh:hh + 1]
        gcc[it] = gc_cols[ci][:, g_heads + hh:g_heads + hh + 1]
        gc_r = gc_rowsl[ci][g_heads + hh:g_heads + hh + 1, :]
        gamma[it] = jnp.exp(jnp.where(incl, gcc[it] - gc_r, -jnp.inf))
        kb = k.astype(BF16)
        qkk = _dot_nt(jnp.concatenate([q.astype(BF16), kb], axis=0), kb)
        qk[it] = qkk[:c]
        xm[it] = -jnp.where(strict, beta[it] * qkk[c:] * gamma[it], 0.0)

    blk_r = [rowi // b for b in (8, 16, 32)]
    blk_c = [coli // b for b in (8, 16, 32)]
    same8 = blk_r[0] == blk_c[0]
    nm = {}
    for it in items:
        x8 = jnp.where(same8, xm[it], 0.0)
        xb = x8.astype(BF16)
        p1 = _dot(xb, xb)
        r = _dot(jnp.concatenate([x8, p1], axis=0).astype(BF16), p1.astype(BF16))
        n3 = x8 + p1 + r[:c]
        p2 = r[c:]
        nm[it] = n3 + p2 + _dot(n3.astype(BF16), p2.astype(BF16))
    for lvl in range(3):
        inner = blk_r[lvl] == blk_c[lvl]
        outer = (rowi // (16 << lvl)) == (coli // (16 << lvl))
        emask = outer & jnp.logical_not(inner)
        for it in items:
            e = jnp.where(emask, -xm[it], 0.0)
            m1 = e + _dot(nm[it].astype(BF16), e.astype(BF16))
            ded = m1 + _dot(m1.astype(BF16), nm[it].astype(BF16))
            nm[it] = nm[it] - ded

    w_c, u_c, attn, q_dec, k_dec, g_last = {}, {}, {}, {}, {}, {}
    for it in items:
        e_gc = jnp.exp(gcc[it])
        rhs = jnp.concatenate([kn[it] * (beta[it] * e_gc), vv[it] * beta[it]], axis=1)
        sol = rhs + _dot(nm[it].astype(BF16), rhs.astype(BF16))
        w_c[it] = sol[:, :HEAD_DIM]
        u_c[it] = sol[:, HEAD_DIM:]
        attn[it] = (qk[it] * gamma[it]).astype(BF16)
        gc_last = gcc[it][c - 1:c, :]
        q_dec[it] = qn[it] * e_gc
        k_dec[it] = (kn[it] * jnp.exp(gc_last - gcc[it])).astype(BF16)
        g_last[it] = jnp.exp(gc_last)

    for it in items:
        ci, hh = it
        sl = slice(ci * c, (ci + 1) * c)
        hs = slice(hh * HEAD_DIM, (hh + 1) * HEAD_DIM)
        st = state_ref[hh]
        ws = _dot(jnp.concatenate([w_c[it], q_dec[it]], axis=0).astype(BF16), st.astype(BF16))
        vnb = (u_c[it] - ws[:c]).astype(BF16)
        o = ws[c:] + _dot(attn[it], vnb)
        state_ref[hh] = st * g_last[it] + _dot_tn(k_dec[it], vnb)
        o = _rms_rows(o) * ng * _silu(z_ref[sl, hs].astype(F32))
        o_ref[sl, hs] = o.astype(o_ref.dtype)


def deltanet_mix(p, conv_w8, small, small_t, crow, ccol, norm_g, batch, seq, n_heads,
                 q_blk, k_blk, v_blk, z_blk, tr=256):
    t = p.shape[0]
    tr = min(tr, seq)
    ns = seq // tr
    ngroups = n_heads // DN_GROUP
    gw = DN_GROUP * HEAD_DIM
    kern = functools.partial(_dn_kernel, tr=tr)
    row = lambda b, g, s: b * ns + s
    return pl.pallas_call(
        kern,
        out_shape=jax.ShapeDtypeStruct((t, n_heads * HEAD_DIM), BF16),
        grid=(batch, ngroups, ns),
        in_specs=[pl.BlockSpec((tr, gw), lambda b, g, s: (row(b, g, s), q_blk + g)),
                  pl.BlockSpec((tr, gw), lambda b, g, s: (row(b, g, s), k_blk + g)),
                  pl.BlockSpec((tr, gw), lambda b, g, s: (row(b, g, s), v_blk + g)),
                  pl.BlockSpec((tr, gw), lambda b, g, s: (row(b, g, s), z_blk + g)),
                  pl.BlockSpec((8, gw), lambda b, g, s: (0, q_blk + g)),
                  pl.BlockSpec((8, gw), lambda b, g, s: (0, k_blk + g)),
                  pl.BlockSpec((8, gw), lambda b, g, s: (0, v_blk + g)),
                  pl.BlockSpec((tr, LANES), lambda b, g, s: (row(b, g, s), g)),
                  pl.BlockSpec((None, None, 8, tr), lambda b, g, s: (b, g, 0, s)),
                  pl.BlockSpec((None, 8, LANES), lambda b, g, s: (g, 0, 0)),
                  pl.BlockSpec((None, 2, 8, LANES), lambda b, g, s: (g, 0, 0, 0)),
                  pl.BlockSpec((1, HEAD_DIM), lambda b, g, s: (0, 0))],
        out_specs=pl.BlockSpec((tr, gw), lambda b, g, s: (row(b, g, s), g)),
        scratch_shapes=[pltpu.VMEM((DN_GROUP, HEAD_DIM, HEAD_DIM), F32),
                        pltpu.VMEM((tr + 8, gw), F32), pltpu.VMEM((tr + 8, gw), F32),
                        pltpu.VMEM((tr + 8, gw), F32),
                        pltpu.VMEM((tr, gw), F32), pltpu.VMEM((tr, gw), F32), pltpu.VMEM((tr, gw), F32)],
        compiler_params=_cparams(3), name="deltanet",
    )(p, p, p, p, conv_w8, conv_w8, conv_w8, small, small_t, crow, ccol, norm_g)


def _norm_router_kernel(x_ref, g_ref, sc_ref, sh_ref, wr_ref, br_ref, h_ref, lg_ref):
    x = x_ref[...]
    h = _rms_rows(x) * g_ref[...] * (1.0 + sc_ref[0]) + sh_ref[0]
    h_ref[...] = h
    w = wr_ref[...]
    w_hi = w.astype(BF16)
    w_lo = (w - w_hi.astype(F32)).astype(BF16)
    h_hi = h.astype(BF16)
    h_lo = (h - h_hi.astype(F32)).astype(BF16)
    lg_ref[...] = _dot(h_hi, w_hi) + _dot(h_lo, w_hi) + _dot(h_hi, w_lo) + br_ref[...]


def norm_router(x2d, g, sc, sh, w_router_pad, b_router_pad, seq, tm=256):
    t, d = x2d.shape
    tm = min(tm, seq)
    nb = seq // tm
    return pl.pallas_call(
        _norm_router_kernel,
        out_shape=(jax.ShapeDtypeStruct((t, d), F32), jax.ShapeDtypeStruct((t, LANES), F32)),
        grid=(t // tm,),
        in_specs=[pl.BlockSpec((tm, d), lambda i: (i, 0)),
                  pl.BlockSpec((1, d), lambda i: (0, 0)),
                  pl.BlockSpec((1, 1, d), lambda i: (i // nb, 0, 0)),
                  pl.BlockSpec((1, 1, d), lambda i: (i // nb, 0, 0)),
                  pl.BlockSpec((d, LANES), lambda i: (0, 0)),
                  pl.BlockSpec((1, LANES), lambda i: (0, 0))],
        out_specs=(pl.BlockSpec((tm, d), lambda i: (i, 0)),
                   pl.BlockSpec((tm, LANES), lambda i: (i, 0))),
        compiler_params=_cparams(1), name="norm_router",
    )(x2d, g.reshape(1, d), sc, sh, w_router_pad, b_router_pad)


def _moe_kernel(blk_e_ref, first_ref, nxt_ref, nused_ref, tok_ref, tokn_ref, h_hbm, wgu_hbm, wd_hbm,
                bgu_ref, bd_ref, y_ref,
                stg_gu, stg_d, wgu_b, wd_b, perm_ref, xbuf0, xbuf1, wsem, gsem, *, tb, layer):
    i = pl.program_id(0)
    nused = nused_ref[0]
    d, f2 = stg_gu.shape
    f = f2 // 2
    xbufs = (xbuf0, xbuf1)

    def weight_copies(e):
        return (pltpu.make_async_copy(wgu_hbm.at[layer, e], stg_gu, wsem.at[0]),
                pltpu.make_async_copy(wd_hbm.at[layer, e], stg_d, wsem.at[1]))

    def row_copy(toks, r, s):
        return pltpu.make_async_copy(h_hbm.at[pl.ds(toks[0, r], 1), :],
                                     xbufs[s].at[pl.ds(r, 1), :], gsem.at[s])

    def wait_gather(s):
        pltpu.make_async_copy(h_hbm.at[pl.ds(0, tb), :], xbufs[s], gsem.at[s]).wait()

    @pl.when(i == 0)
    def _():
        for cp in weight_copies(blk_e_ref[0]):
            cp.start()

        def issue(r, carry):
            row_copy(tok_ref, r, 0).start()
            return carry
        lax.fori_loop(0, tb, issue, 0)
        rr = lax.broadcasted_iota(jnp.int32, (f2, f), 0)
        cc = lax.broadcasted_iota(jnp.int32, (f2, f), 1)
        perm_ref[...] = jnp.where(rr == 2 * cc, 1.0, 0.0).astype(BF16)

    @pl.when((i < nused) & (first_ref[i] == 1))
    def _():
        for cp in weight_copies(0):
            cp.wait()
        rows = min(256, f)

        def cast_gu(r, carry):
            r0 = pl.multiple_of(r * rows, rows)
            wgu_b[pl.ds(r0, rows), :] = stg_gu[pl.ds(r0, rows), :].astype(BF16)
            return carry
        lax.fori_loop(0, d // rows, cast_gu, 0)

        def cast_d(r, carry):
            r0 = pl.multiple_of(r * rows, rows)
            wd_b[pl.ds(r0, rows), :] = stg_d[pl.ds(r0, rows), :].astype(BF16)
            return carry
        lax.fori_loop(0, f // rows, cast_d, 0)

        @pl.when(nxt_ref[i] >= 0)
        def _():
            for cp in weight_copies(nxt_ref[i]):
                cp.start()

    def compute(s):
        for r in range(tb):
            row_copy(tokn_ref, r, 1 - s).start()
        wait_gather(s)
        x = xbufs[s][...].astype(BF16)
        gu = _dot(x, wgu_b[...]) + bgu_ref[...]
        gate = jnp.minimum(gu, SWIGLU_LIMIT)
        sg = gate * _sigmoid(SWIGLU_ALPHA * gate)
        lin1 = jnp.clip(gu, -SWIGLU_LIMIT, SWIGLU_LIMIT) + 1.0
        act_il = sg * pltpu.roll(lin1, f2 - 1, 1)
        act = _dot(act_il.astype(BF16), perm_ref[...])
        y = _dot(act.astype(BF16), wd_b[...]) + bd_ref[...]
        y_ref[...] = _bf16_bits_lo(y[:, :d // 2]) | _bf16_bits_hi(y[:, d // 2:])

    for s in range(2):
        @pl.when((i < nused) & (i % 2 == s))
        def _(s=s):
            compute(s)

        @pl.when((i == nused) & (i % 2 == s))
        def _(s=s):
            wait_gather(s)

    @pl.when(i >= nused)
    def _():
        y_ref[...] = jnp.zeros_like(y_ref)


def moe_experts(h2, w_gu, w_down, b_gu_l, b_down_l, layer, blk_e, first, nxt, n_used, row_tok, tb):
    t, d = h2.shape
    n_blk = row_tok.shape[0]
    f2 = w_gu.shape[3]
    f = f2 // 2
    kern = functools.partial(_moe_kernel, tb=tb, layer=layer)
    grid_spec = pltpu.PrefetchScalarGridSpec(
        num_scalar_prefetch=4,
        grid=(n_blk,),
        in_specs=[pl.BlockSpec((None, 1, tb), lambda i, *_: (i, 0, 0), memory_space=pltpu.SMEM),
                  pl.BlockSpec((None, 1, tb), lambda i, *_: (jnp.minimum(i + 1, n_blk - 1), 0, 0),
                               memory_space=pltpu.SMEM),
                  pl.BlockSpec(memory_space=pl.ANY),
                  pl.BlockSpec(memory_space=pl.ANY),
                  pl.BlockSpec(memory_space=pl.ANY),
                  pl.BlockSpec((None, 1, f2), lambda i, be, *_: (be[i], 0, 0)),
                  pl.BlockSpec((None, 1, d), lambda i, be, *_: (be[i], 0, 0))],
        out_specs=pl.BlockSpec((tb, d // 2), lambda i, *_: (i, 0)),
        scratch_shapes=[pltpu.VMEM((d, f2), F32), pltpu.VMEM((f, d), F32),
                        pltpu.VMEM((d, f2), BF16), pltpu.VMEM((f, d), BF16),
                        pltpu.VMEM((f2, f), BF16), pltpu.VMEM((tb, d), F32), pltpu.VMEM((tb, d), F32),
                        pltpu.SemaphoreType.DMA((2,)), pltpu.SemaphoreType.DMA((2,))])
    return pl.pallas_call(
        kern,
        out_shape=jax.ShapeDtypeStruct((n_blk * tb, d // 2), U32),
        grid_spec=grid_spec,
        compiler_params=_cparams(1), name="moe_experts",
    )(blk_e, first, nxt, n_used, row_tok, row_tok, h2, w_gu, w_down, b_gu_l, b_down_l)


def _combine_kernel(pos_ref, posn_ref, y_hbm, x_ref, tw_ref, g2_ref, ng_ref, sc_ref, sh_ref, *rest,
                    tm, final, n_steps):
    if final:
        out_ref, ybuf0, ybuf1, sem = rest
    else:
        xo_ref, h_ref, ybuf0, ybuf1, sem = rest
    i = pl.program_id(0)
    ybufs = (ybuf0, ybuf1)

    def start_gather(p_ref, s):
        def issue(r, carry):
            for k in range(TOP_K):
                pltpu.make_async_copy(y_hbm.at[pl.ds(p_ref[0, r * TOP_K + k], 1), :],
                                      ybufs[s].at[k, pl.ds(r, 1), :], sem.at[s]).start()
            return carry
        lax.fori_loop(0, tm, issue, 0)

    def wait_gather(s):
        for k in range(TOP_K):
            pltpu.make_async_copy(y_hbm.at[pl.ds(0, tm), :], ybufs[s].at[k], sem.at[s]).wait()

    def lo(w):
        return pltpu.bitcast(w << 16, F32)

    def hi(w):
        return pltpu.bitcast(w & jnp.uint32(HI16), F32)

    def finish(s):
        tw = tw_ref[...]
        acc_lo = acc_hi = None
        for k in range(TOP_K):
            w = ybufs[s][k]
            wk = tw[:, k:k + 1]
            acc_lo = wk * lo(w) if acc_lo is None else acc_lo + wk * lo(w)
            acc_hi = wk * hi(w) if acc_hi is None else acc_hi + wk * hi(w)
        x = x_ref[...] + g2_ref[0] * jnp.concatenate([acc_lo, acc_hi], axis=1)
        if final:
            out_ref[...] = _rms_rows(x) * ng_ref[...]
        else:
            xo_ref[...] = x
            h = _rms_rows(x) * ng_ref[...] * (1.0 + sc_ref[0]) + sh_ref[0]
            h_ref[...] = h.astype(h_ref.dtype)

    @pl.when(i == 0)
    def _():
        start_gather(pos_ref, 0)

    for s in range(2):
        @pl.when(i % 2 == s)
        def _(s=s):
            @pl.when(i + 1 < n_steps)
            def _():
                start_gather(posn_ref, 1 - s)
            wait_gather(s)
            finish(s)


def moe_combine(y, pos, top_w_pad, x2d, gate2, norm_g, sc, sh, seq, final, tm=128):
    t, d = x2d.shape
    tm = min(tm, seq)
    nb = seq // tm
    n_steps = t // tm
    kern = functools.partial(_combine_kernel, tm=tm, final=final, n_steps=n_steps)
    if final:
        out_shape = jax.ShapeDtypeStruct((t, d), F32)
        out_specs = pl.BlockSpec((tm, d), lambda i: (i, 0))
    else:
        out_shape = (jax.ShapeDtypeStruct((t, d), F32), jax.ShapeDtypeStruct((t, d), BF16))
        out_specs = (pl.BlockSpec((tm, d), lambda i: (i, 0)), pl.BlockSpec((tm, d), lambda i: (i, 0)))
    pos3 = pos.reshape(n_steps, 1, tm * TOP_K)
    return pl.pallas_call(
        kern,
        out_shape=out_shape,
        grid=(n_steps,),
        in_specs=[pl.BlockSpec((None, 1, tm * TOP_K), lambda i: (i, 0, 0), memory_space=pltpu.SMEM),
                  pl.BlockSpec((None, 1, tm * TOP_K), lambda i: (jnp.minimum(i + 1, n_steps - 1), 0, 0),
                               memory_space=pltpu.SMEM),
                  pl.BlockSpec(memory_space=pl.ANY),
                  pl.BlockSpec((tm, d), lambda i: (i, 0)),
                  pl.BlockSpec((tm, LANES), lambda i: (i, 0)),
                  pl.BlockSpec((1, 1, d), lambda i: (i // nb, 0, 0)),
                  pl.BlockSpec((1, d), lambda i: (0, 0)),
                  pl.BlockSpec((1, 1, d), lambda i: (i // nb, 0, 0)),
                  pl.BlockSpec((1, 1, d), lambda i: (i // nb, 0, 0))],
        out_specs=out_specs,
        scratch_shapes=[pltpu.VMEM((TOP_K, tm, d // 2), U32), pltpu.VMEM((TOP_K, tm, d // 2), U32),
                        pltpu.SemaphoreType.DMA((2,))],
        compiler_params=_cparams(1), name="moe_combine",
    )(pos3, pos3, y, x2d, top_w_pad, gate2, norm_g.reshape(1, d), sc, sh)


def _routing_tables(logits, n_experts, tb):
    t = logits.shape[0]
    top_v, top_e = lax.top_k(logits[:, :n_experts], TOP_K)
    top_w = jax.nn.softmax(top_v, axis=-1)
    n_a = t * TOP_K
    eids = jnp.arange(n_experts, dtype=jnp.int32)
    sel = jnp.any(top_e[:, :, None] == eids[None, None, :], axis=1).astype(jnp.int32)
    csum = jnp.cumsum(sel, axis=0)
    counts = csum[-1]
    rank = jnp.take_along_axis(csum - sel, top_e, axis=1)
    padded = (counts + tb - 1) // tb * tb
    pend = jnp.cumsum(padded)
    pstart = pend - padded
    pos = (pstart[top_e] + rank).astype(jnp.int32).reshape(n_a)
    n_rows = (n_a + n_experts * (tb - 1) + tb - 1) // tb * tb + tb
    n_blk = n_rows // tb
    t_a = jnp.repeat(jnp.arange(t, dtype=jnp.int32), TOP_K)
    row_tok = jnp.zeros((n_rows,), jnp.int32).at[pos].set(t_a)
    blk_start = jnp.arange(n_blk, dtype=jnp.int32) * tb
    blk_e = jnp.minimum(jnp.sum((pend[None, :] <= blk_start[:, None]).astype(jnp.int32), axis=1),
                        n_experts - 1).astype(jnp.int32)
    n_used = (pend[-1] // tb).astype(jnp.int32)
    used = jnp.arange(n_blk) < n_used
    first = (used & (blk_start == pstart[blk_e])).astype(jnp.int32)
    has = counts > 0
    cand = jnp.where(has, eids, n_experts)
    suffix_min = lax.cummin(cand, axis=0, reverse=True)
    nxt_e = jnp.concatenate([suffix_min[1:], jnp.full((1,), n_experts, jnp.int32)])
    nxt_e = jnp.where(nxt_e >= n_experts, -1, nxt_e).astype(jnp.int32)
    nxt = nxt_e[blk_e]
    top_w_pad = jnp.pad(top_w, ((0, 0), (0, LANES - TOP_K)))
    return pos, top_w_pad, row_tok.reshape(n_blk, 1, tb), blk_e, first, nxt, n_used.reshape(1)


def _rope_tables(seq):
    half = HEAD_DIM // 2
    inv = ROPE_BASE ** (-jnp.arange(half, dtype=F32) / half)
    ang = jnp.arange(seq, dtype=F32)[:, None] * inv[None, :]
    cos, sin = jnp.cos(ang), jnp.sin(ang)
    return jnp.concatenate([cos, cos], axis=-1), jnp.concatenate([-sin, sin], axis=-1)


def kernel(x, c, norm1_g, w_ada, b_ada, w_in, dn_conv_w, dn_a_log, dn_dt_bias, dn_norm_g, fox_f_bias,
           w_out, norm2_g, w_router, b_router, w_gu, b_gu, w_down, b_down, final_norm_g):
    batch, seq, d = x.shape
    depth = w_ada.shape[0]
    t = batch * seq
    dn_heads = dn_a_log.shape[1]
    fox_heads = fox_f_bias.shape[1]
    ret_heads = d // HEAD_DIM - dn_heads - fox_heads
    dn_w, ret_w, fox_w = dn_heads * HEAD_DIM, ret_heads * HEAD_DIM, fox_heads * HEAD_DIM
    n_experts = w_router.shape[2]
    n_groups = dn_heads // DN_GROUP
    gw = DN_GROUP * HEAD_DIM
    moe_tb = 256
    tn = 512

    widths = (3 * dn_w, dn_w, dn_heads, dn_heads, ret_w, ret_w, ret_w, ret_w,
              fox_w, fox_w, fox_w, fox_w, fox_heads)
    cuts = np.concatenate([[0], np.cumsum(widths)])
    o_b, o_a, o_r, o_ff = int(cuts[2]), int(cuts[3]), int(cuts[4]), int(cuts[12])
    n_main = o_b + (o_ff - o_r)
    small_src, small_dst = [], []
    for g in range(n_groups):
        for hh in range(DN_GROUP):
            small_src += [g * DN_GROUP + hh, dn_heads + g * DN_GROUP + hh]
            small_dst += [g * LANES + hh, g * LANES + DN_GROUP + hh]
    for hh in range(fox_heads):
        small_src.append(2 * dn_heads + hh)
        small_dst.append(n_groups * LANES + hh)
    small_src = np.asarray(small_src, np.int32)
    small_dst = np.asarray(small_dst, np.int32)
    w_in_t = jnp.transpose(w_in, (2, 0, 1))
    w_main_t = w_in_prep(w_in_t, o_b, o_r - o_b, n_main)
    w_sc = jnp.concatenate([w_in_t[o_b:o_r], w_in_t[o_ff:]], axis=0)
    w_small_t = jnp.zeros((depth, (n_groups + 1) * LANES, d), BF16).at[:, small_dst].set(
        jnp.transpose(w_sc, (1, 0, 2))[:, small_src].astype(BF16))

    x2d = x.reshape(t, d)
    c_pad = jnp.zeros((8, d), BF16).at[:batch].set(c.astype(BF16))
    mod = ada_mod(c_pad, w_ada, b_ada)[:, :batch, :]
    mod = mod.reshape(depth, batch, 6, 1, d)
    cos2, sin2 = _rope_tables(seq)
    lg_tab = jnp.log(1.0 - 2.0 ** (-5.0 - jnp.arange(ret_heads, dtype=F32)))
    lg_tab = jnp.broadcast_to(lg_tab[:, None, None], (ret_heads, 8, LANES))

    dn_q_blk, dn_k_blk, dn_v_blk, dn_z_blk = 0, dn_w // gw, 2 * dn_w // gw, 3 * dn_w // gw
    r0 = o_b // HEAD_DIM
    r_q_blk, r_k_blk, r_v_blk, r_g_blk = r0, r0 + ret_heads, r0 + 2 * ret_heads, r0 + 3 * ret_heads
    f0 = r0 + 4 * ret_heads
    f_q_blk, f_k_blk, f_v_blk, f_z_blk = f0, f0 + fox_heads, f0 + 2 * fox_heads, f0 + 3 * fox_heads

    h = norm_mod(x2d, norm1_g[0], mod[0, :, 1], mod[0, :, 0], seq)
    out = None
    for l in range(depth):
        sh1, sc1, g1, sh2, sc2, g2 = (mod[l, :, i] for i in range(6))
        p = matmul_nt(h, w_main_t, l, BF16, "in_proj")
        small = matmul_nt(h, w_small_t, l, F32, "small_proj")

        conv_w8 = jnp.zeros((8, 3 * dn_w), F32).at[:CONV_K].set(dn_conv_w[l])
        small_t = small[:, :n_groups * LANES].reshape(batch, seq, n_groups, LANES)[..., :8]
        small_t = small_t.transpose(0, 2, 3, 1)
        dtb = dn_dt_bias[l].reshape(n_groups, DN_GROUP)
        nega = -jnp.exp(dn_a_log[l]).reshape(n_groups, DN_GROUP)
        zg = jnp.zeros((n_groups, DN_GROUP), F32)
        pad = jnp.zeros((n_groups, LANES - 2 * DN_GROUP), F32)
        crow = jnp.stack([jnp.concatenate([zg, dtb, pad], axis=1),
                          jnp.concatenate([zg, nega, pad], axis=1)], axis=1)
        crow = jnp.concatenate([crow, jnp.zeros((n_groups, 6, LANES), F32)], axis=1)
        ccol = jnp.stack([jnp.concatenate([zg, dtb], axis=1), jnp.concatenate([zg, nega], axis=1)], axis=1)
        ccol = jnp.broadcast_to(ccol[..., None], (n_groups, 2, 8, LANES))
        o_dn = deltanet_mix(p, conv_w8, small, small_t, crow, ccol, dn_norm_g[l].reshape(1, HEAD_DIM),
                            batch, seq, dn_heads, dn_q_blk, dn_k_blk, dn_v_blk, dn_z_blk)
        o_ret = retention_mix(p, cos2, sin2, lg_tab, batch, seq, ret_heads,
                              r_q_blk, r_k_blk, r_v_blk, r_g_blk)
        fb = jnp.zeros((1, LANES), F32).at[0, :fox_heads].set(fox_f_bias[l])
        f_cum = fox_prefix(small, fb, batch, seq, n_groups)
        o_fox = fox_attention(p, f_cum, batch, seq, fox_heads, f_q_blk, f_k_blk, f_v_blk, f_z_blk)
        x2d = out_proj_resid(o_dn, o_ret, o_fox, w_out, l, x2d, g1, seq)

        wr_pad = jnp.zeros((d, LANES), F32).at[:, :n_experts].set(w_router[l])
        br_pad = jnp.zeros((1, LANES), F32).at[0, :n_experts].set(b_router[l])
        h2, logits = norm_router(x2d, norm2_g[l], sc2, sh2, wr_pad, br_pad, seq)
        pos, top_w_pad, row_tok, blk_e, first, nxt, n_used = _routing_tables(logits, n_experts, moe_tb)
        y = moe_experts(h2, w_gu, w_down, b_gu[l][:, None, :], b_down[l][:, None, :], l,
                        blk_e, first, nxt, n_used, row_tok, moe_tb)
        if l + 1 < depth:
            nsh1, nsc1 = mod[l + 1, :, 0], mod[l + 1, :, 1]
            x2d, h = moe_combine(y, pos, top_w_pad, x2d, g2, norm1_g[l + 1], nsc1, nsh1, seq, final=False)
        else:
            out = moe_combine(y, pos, top_w_pad, x2d, g2, final_norm_g, sc2, sh2, seq, final=True)
    return out.reshape(batch, seq, d)
```

```python
import functools

import numpy as np
import jax
import jax.numpy as jnp
from jax import lax
from jax.experimental import pallas as pl
from jax.experimental.pallas import tpu as pltpu

F32 = jnp.float32
BF16 = jnp.bfloat16
U32 = jnp.uint32

HEAD_DIM = 128
CONV_K = 4
DN_CHUNK = 128
RET_CHUNK = 128
ROPE_BASE = 10000.0
TOP_K = 4
SWIGLU_LIMIT = 7.0
SWIGLU_ALPHA = 1.702
EPS = 1e-6
DN_GROUP = 4
LANES = 128
VMEM_LIMIT = 56 * 1024 * 1024
LOG2E = 1.4426950408889634
HI16 = 0xFFFF0000


def _cparams(n_axes):
    return pltpu.CompilerParams(dimension_semantics=("arbitrary",) * n_axes,
                                vmem_limit_bytes=VMEM_LIMIT)


def _dot(a, b):
    return jnp.dot(a, b, preferred_element_type=F32)


def _dot_nt(a, b):
    return lax.dot_general(a, b, (((1,), (1,)), ((), ())), preferred_element_type=F32)


def _dot_tn(a, b):
    return lax.dot_general(a, b, (((0,), (0,)), ((), ())), preferred_element_type=F32)


def _split3(x):
    h = x.astype(BF16)
    r = x - h.astype(F32)
    m = r.astype(BF16)
    l = (r - m.astype(F32)).astype(BF16)
    return h, m, l


def _sigmoid(x):
    return 1.0 / (1.0 + jnp.exp(-x))


def _silu(x):
    return x * _sigmoid(x)


def _softplus(x):
    return jnp.maximum(x, 0.0) + jnp.log1p(jnp.exp(-jnp.abs(x)))


def _log_sigmoid(x):
    return jnp.minimum(x, 0.0) - jnp.log1p(jnp.exp(-jnp.abs(x)))


def _rms_rows(x):
    return x * lax.rsqrt(jnp.mean(x * x, axis=-1, keepdims=True) + EPS)


def _bf16_bits_lo(x):
    return pltpu.bitcast(x.astype(BF16).astype(F32), U32) >> 16


def _bf16_bits_hi(x):
    return pltpu.bitcast(x.astype(BF16).astype(F32), U32) & jnp.uint32(HI16)


def _ada_kernel(c_ref, w_ref, b_ref, o_ref):
    o_ref[...] = _dot(c_ref[...], w_ref[...].astype(BF16)) + b_ref[...]


def ada_mod(c_pad, w_ada, b_ada, tn=512):
    depth, d, n = w_ada.shape
    return pl.pallas_call(
        _ada_kernel,
        out_shape=jax.ShapeDtypeStruct((depth, 8, n), F32),
        grid=(depth, n // tn),
        in_specs=[pl.BlockSpec((8, d), lambda l, j: (0, 0)),
                  pl.BlockSpec((None, d, tn), lambda l, j: (l, 0, j)),
                  pl.BlockSpec((None, 1, tn), lambda l, j: (l, 0, j))],
        out_specs=pl.BlockSpec((None, 8, tn), lambda l, j: (l, 0, j)),
        compiler_params=_cparams(2), name="ada_mod",
    )(c_pad, w_ada, b_ada.reshape(depth, 1, n))


def _norm_mod_kernel(x_ref, g_ref, sc_ref, sh_ref, h_ref):
    x = x_ref[...]
    h = _rms_rows(x) * g_ref[...] * (1.0 + sc_ref[0]) + sh_ref[0]
    h_ref[...] = h.astype(h_ref.dtype)


def norm_mod(x2d, g, sc, sh, seq, tm=256):
    t, d = x2d.shape
    tm = min(tm, seq)
    nb = seq // tm
    return pl.pallas_call(
        _norm_mod_kernel,
        out_shape=jax.ShapeDtypeStruct((t, d), BF16),
        grid=(t // tm,),
        in_specs=[pl.BlockSpec((tm, d), lambda i: (i, 0)),
                  pl.BlockSpec((1, d), lambda i: (0, 0)),
                  pl.BlockSpec((1, 1, d), lambda i: (i // nb, 0, 0)),
                  pl.BlockSpec((1, 1, d), lambda i: (i // nb, 0, 0))],
        out_specs=pl.BlockSpec((tm, d), lambda i: (i, 0)),
        compiler_params=_cparams(1), name="norm_mod",
    )(x2d, g.reshape(1, d), sc, sh)


def _w_prep_kernel(w_ref, o_ref):
    for l in range(o_ref.shape[0]):
        o_ref[l] = w_ref[:, l, :].astype(BF16)


def w_in_prep(w_in_t, gap_start, gap_len, n_rows, rows=256):
    n, depth, k = w_in_t.shape
    rows = min(rows, gap_start)
    assert gap_start % rows == 0 and n_rows % rows == 0 and gap_len % 8 == 0

    def in_idx(j):
        r = j * rows
        return (r + jnp.where(r >= gap_start, gap_len, 0), 0, 0)
    return pl.pallas_call(
        _w_prep_kernel,
        out_shape=jax.ShapeDtypeStruct((depth, n_rows, k), BF16),
        grid=(n_rows // rows,),
        in_specs=[pl.BlockSpec((pl.Element(rows), pl.Element(depth), pl.Element(k)), in_idx)],
        out_specs=pl.BlockSpec((depth, rows, k), lambda j: (0, j, 0)),
        compiler_params=_cparams(1), name="w_in_prep",
    )(w_in_t)


def _mm_nt_kernel(x_ref, w_ref, o_ref):
    o_ref[...] = _dot_nt(x_ref[...], w_ref[...]).astype(o_ref.dtype)


def matmul_nt(x, w_t, layer, out_dtype, name, tm=1024, tn=512):
    m, k = x.shape
    n = w_t.shape[1]
    tm = min(tm, m)
    tn = min(tn, n)
    return pl.pallas_call(
        _mm_nt_kernel,
        out_shape=jax.ShapeDtypeStruct((m, n), out_dtype),
        grid=(n // tn, m // tm),
        in_specs=[pl.BlockSpec((tm, k), lambda j, i: (i, 0)),
                  pl.BlockSpec((None, tn, k), lambda j, i: (layer, j, 0))],
        out_specs=pl.BlockSpec((tm, tn), lambda j, i: (i, j)),
        compiler_params=_cparams(2), name=name,
    )(x, w_t)


def _mm_resid_kernel(a_ref, b_ref, c_ref, w_ref, x_ref, g_ref, o_ref, wb_ref, *, splits):
    @pl.when(pl.program_id(1) == 0)
    def _():
        wb_ref[...] = w_ref[...].astype(BF16)
    k0, k1 = splits
    y = (_dot(a_ref[...], wb_ref[0:k0, :]) + _dot(b_ref[...], wb_ref[k0:k1, :])
         + _dot(c_ref[...], wb_ref[k1:, :]))
    o_ref[...] = x_ref[...] + g_ref[0] * y


def out_proj_resid(a, b, c, w_out, layer, x2d, gate, seq, tm=1024, tn=512):
    m = a.shape[0]
    ka, kb, kc = a.shape[1], b.shape[1], c.shape[1]
    k = ka + kb + kc
    n = w_out.shape[2]
    tm = min(tm, seq)
    nb = seq // tm
    kern = functools.partial(_mm_resid_kernel, splits=(ka, ka + kb))
    return pl.pallas_call(
        kern,
        out_shape=jax.ShapeDtypeStruct((m, n), F32),
        grid=(n // tn, m // tm),
        in_specs=[pl.BlockSpec((tm, ka), lambda j, i: (i, 0)),
                  pl.BlockSpec((tm, kb), lambda j, i: (i, 0)),
                  pl.BlockSpec((tm, kc), lambda j, i: (i, 0)),
                  pl.BlockSpec((None, k, tn), lambda j, i: (layer, 0, j)),
                  pl.BlockSpec((tm, tn), lambda j, i: (i, j)),
                  pl.BlockSpec((1, 1, tn), lambda j, i: (i // nb, 0, j))],
        out_specs=pl.BlockSpec((tm, tn), lambda j, i: (i, j)),
        scratch_shapes=[pltpu.VMEM((k, tn), BF16)],
        compiler_params=_cparams(2), name="out_proj",
    )(a, b, c, w_out, x2d, gate)


def _fprep_kernel(s_ref, b_ref, f_ref, carry_ref):
    @pl.when(pl.program_id(1) == 0)
    def _():
        carry_ref[...] = jnp.zeros_like(carry_ref)
    tq = s_ref.shape[0]
    lf = _log_sigmoid(s_ref[...] + b_ref[...])
    row = lax.broadcasted_iota(jnp.int32, (tq, tq), 0)
    col = lax.broadcasted_iota(jnp.int32, (tq, tq), 1)
    tri = jnp.where(row >= col, 1.0, 0.0).astype(BF16)
    h, m, l = _split3(lf)
    f = _dot(tri, h) + _dot(tri, m) + _dot(tri, l) + carry_ref[0:1, :]
    f_ref[...] = f
    carry_ref[...] = jnp.broadcast_to(f[tq - 1:tq, :], carry_ref.shape)


def fox_prefix(small, bias_row, batch, seq, col_block, tq=512):
    t = small.shape[0]
    tq = min(tq, seq)
    nq = seq // tq
    return pl.pallas_call(
        _fprep_kernel,
        out_shape=jax.ShapeDtypeStruct((t, LANES), F32),
        grid=(batch, nq),
        in_specs=[pl.BlockSpec((tq, LANES), lambda b, i: (b * nq + i, col_block)),
                  pl.BlockSpec((1, LANES), lambda b, i: (0, 0))],
        out_specs=pl.BlockSpec((tq, LANES), lambda b, i: (b * nq + i, 0)),
        scratch_shapes=[pltpu.VMEM((8, LANES), F32)],
        compiler_params=_cparams(2), name="fox_prefix",
    )(small, bias_row)


def _fox_kernel(q_ref, k_ref, v_ref, z_ref, f_ref, o_ref, ka_ref, va_ref, m_ref, acc_ref, *, tq, seq):
    h = pl.program_id(1)
    qi = pl.program_id(2)
    lane = lax.broadcasted_iota(jnp.int32, (tq, LANES), 1)

    def f_column(rows):
        fb = f_ref[pl.ds(rows, tq), :]
        return jnp.sum(jnp.where(lane == h, fb, 0.0), axis=-1, keepdims=True) * LOG2E

    def aug_lanes(fcol, sign_first):
        fh, fm, fl = _split3(fcol)
        fh, fm, fl = fh.astype(F32), fm.astype(F32), fl.astype(F32)
        one = jnp.ones((tq, LANES), F32)
        zero = jnp.zeros((tq, LANES), F32)
        if sign_first:
            a = jnp.where(lane == 0, fh, jnp.where(lane == 1, fm, jnp.where(lane == 2, fl,
                          jnp.where(lane < 6, one, zero))))
        else:
            a = jnp.where(lane < 3, one, jnp.where(lane == 3, -fh, jnp.where(lane == 4, -fm,
                          jnp.where(lane == 5, -fl, zero))))
        return a.astype(BF16)

    @pl.when(qi == 0)
    def _():
        def build(j, carry):
            r0 = pl.multiple_of(j * tq, tq)
            ka_ref[pl.ds(r0, tq), 0:HEAD_DIM] = k_ref[pl.ds(r0, tq), :]
            ka_ref[pl.ds(r0, tq), HEAD_DIM:] = aug_lanes(f_column(r0), False)
            va_ref[pl.ds(r0, tq), 0:HEAD_DIM] = v_ref[pl.ds(r0, tq), :]
            va_ref[pl.ds(r0, tq), HEAD_DIM:] = jnp.ones((tq, HEAD_DIM), BF16)
            return carry
        lax.fori_loop(0, seq // tq, build, 0)

    q0 = pl.multiple_of(qi * tq, tq)
    qs = (q_ref[...].astype(F32) * (HEAD_DIM ** -0.5 * LOG2E)).astype(BF16)
    qa = jnp.concatenate([qs, aug_lanes(f_column(q0), True)], axis=1)

    m_ref[...] = jnp.full_like(m_ref, -1e30)
    acc_ref[...] = jnp.zeros_like(acc_ref)

    def block(k0, width, masked):
        s = _dot_nt(qa, ka_ref[pl.ds(k0, width), :])
        if masked:
            row = lax.broadcasted_iota(jnp.int32, (tq, width), 0)
            col = lax.broadcasted_iota(jnp.int32, (tq, width), 1)
            s = jnp.where(row >= col, s, -1e30)
        m_old = m_ref[...]
        m_new = jnp.maximum(m_old, jnp.max(s, axis=-1, keepdims=True))
        p = jnp.exp2(s - m_new).astype(BF16)
        acc_ref[...] = jnp.exp2(m_old - m_new) * acc_ref[...] + _dot(p, va_ref[pl.ds(k0, width), :])
        m_ref[...] = m_new

    def wide(j, carry):
        block(pl.multiple_of(j * (2 * tq), 2 * tq), 2 * tq, False)
        return carry
    lax.fori_loop(0, qi // 2, wide, 0)

    @pl.when(qi % 2 == 1)
    def _():
        block(pl.multiple_of((qi - 1) * tq, tq), tq, False)

    block(q0, tq, True)
    acc = acc_ref[...]
    o = acc[:, :HEAD_DIM] / acc[:, HEAD_DIM:]
    o_ref[...] = (o * _sigmoid(z_ref[...].astype(F32))).astype(o_ref.dtype)


def fox_attention(p, f_cum, batch, seq, n_heads, q_blk, k_blk, v_blk, z_blk, tq=512):
    t = p.shape[0]
    tq = min(tq, seq)
    nq = seq // tq
    kern = functools.partial(_fox_kernel, tq=tq, seq=seq)
    return pl.pallas_call(
        kern,
        out_shape=jax.ShapeDtypeStruct((t, n_heads * HEAD_DIM), BF16),
        grid=(batch, n_heads, nq),
        in_specs=[pl.BlockSpec((tq, HEAD_DIM), lambda b, h, i: (b * nq + i, q_blk + h)),
                  pl.BlockSpec((seq, HEAD_DIM), lambda b, h, i: (b, k_blk + h)),
                  pl.BlockSpec((seq, HEAD_DIM), lambda b, h, i: (b, v_blk + h)),
                  pl.BlockSpec((tq, HEAD_DIM), lambda b, h, i: (b * nq + i, z_blk + h)),
                  pl.BlockSpec((seq, LANES), lambda b, h, i: (b, 0))],
        out_specs=pl.BlockSpec((tq, HEAD_DIM), lambda b, h, i: (b * nq + i, h)),
        scratch_shapes=[pltpu.VMEM((seq, 2 * HEAD_DIM), BF16), pltpu.VMEM((seq, 2 * HEAD_DIM), BF16),
                        pltpu.VMEM((tq, 1), F32), pltpu.VMEM((tq, 2 * HEAD_DIM), F32)],
        compiler_params=_cparams(3), name="fox_attn",
    )(p, p, p, p, f_cum)


def _ret_kernel(q_ref, k_ref, v_ref, g_ref, cos_ref, sin_ref, lg_ref, o_ref, r_ref, *, tc):
    c = RET_CHUNK

    @pl.when(pl.program_id(2) == 0)
    def _():
        r_ref[...] = jnp.zeros_like(r_ref)

    lg = lg_ref[0:1, 0:1]
    rowi = lax.broadcasted_iota(jnp.int32, (c, c), 0)
    coli = lax.broadcasted_iota(jnp.int32, (c, c), 1)
    diff = (rowi - coli).astype(F32)
    decay = jnp.where(diff >= 0, jnp.exp(lg * jnp.maximum(diff, 0.0)), 0.0)
    pos = lax.broadcasted_iota(jnp.int32, (c, 1), 0).astype(F32)
    q_scale = jnp.exp(lg * (pos + 1.0))
    k_scale = jnp.exp(lg * (c - 1.0 - pos))
    chunk_decay = jnp.exp(lg * c)

    def rope(t, cos2, sin2):
        return t * cos2 + pltpu.roll(t, HEAD_DIM // 2, 1) * sin2

    for ci in range(tc // c):
        sl = slice(ci * c, (ci + 1) * c)
        cos2 = cos_ref[sl, :]
        sin2 = sin_ref[sl, :]
        q = rope(q_ref[sl, :].astype(F32), cos2, sin2)
        k = rope(k_ref[sl, :].astype(F32), cos2, sin2) * (HEAD_DIM ** -0.5)
        v = v_ref[sl, :]
        inner = _dot_nt(q.astype(BF16), k.astype(BF16)) * decay
        r_prev = r_ref[...]
        o = _dot(inner.astype(BF16), v) + _dot((q * q_scale).astype(BF16), r_prev.astype(BF16))
        r_ref[...] = r_prev * chunk_decay + _dot_tn((k * k_scale).astype(BF16), v)
        o = _rms_rows(o) * _silu(g_ref[sl, :].astype(F32))
        o_ref[sl, :] = o.astype(o_ref.dtype)


def retention_mix(p, cos2, sin2, lg_tab, batch, seq, n_heads, q_blk, k_blk, v_blk, g_blk, tc=512):
    t = p.shape[0]
    tc = min(tc, seq)
    ns = seq // tc
    kern = functools.partial(_ret_kernel, tc=tc)
    return pl.pallas_call(
        kern,
        out_shape=jax.ShapeDtypeStruct((t, n_heads * HEAD_DIM), BF16),
        grid=(batch, n_heads, ns),
        in_specs=[pl.BlockSpec((tc, HEAD_DIM), lambda b, h, s: (b * ns + s, q_blk + h)),
                  pl.BlockSpec((tc, HEAD_DIM), lambda b, h, s: (b * ns + s, k_blk + h)),
                  pl.BlockSpec((tc, HEAD_DIM), lambda b, h, s: (b * ns + s, v_blk + h)),
                  pl.BlockSpec((tc, HEAD_DIM), lambda b, h, s: (b * ns + s, g_blk + h)),
                  pl.BlockSpec((tc, HEAD_DIM), lambda b, h, s: (s, 0)),
                  pl.BlockSpec((tc, HEAD_DIM), lambda b, h, s: (s, 0)),
                  pl.BlockSpec((None, 8, LANES), lambda b, h, s: (h, 0, 0))],
        out_specs=pl.BlockSpec((tc, HEAD_DIM), lambda b, h, s: (b * ns + s, h)),
        scratch_shapes=[pltpu.VMEM((HEAD_DIM, HEAD_DIM), F32)],
        compiler_params=_cparams(3), name="retention",
    )(p, p, p, p, cos2, sin2, lg_tab)


def _dn_kernel(uq_ref, uk_ref, uv_ref, z_ref, cwq_ref, cwk_ref, cwv_ref, sm_ref, smt_ref,
               crow_ref, ccol_ref, ng_ref, o_ref,
               state_ref, bq_ref, bk_ref, bv_ref, qs_ref, ks_ref, vs_ref, *, tr):
    c = DN_CHUNK
    g_heads = DN_GROUP
    gw = g_heads * HEAD_DIM

    @pl.when(pl.program_id(2) == 0)
    def _():
        state_ref[...] = jnp.zeros_like(state_ref)
        bq_ref[0:8, :] = jnp.zeros((8, gw), F32)
        bk_ref[0:8, :] = jnp.zeros((8, gw), F32)
        bv_ref[0:8, :] = jnp.zeros((8, gw), F32)

    def conv(u_ref, cw_ref, buf_ref, dst_ref):
        buf_ref[8:8 + tr, :] = u_ref[...].astype(F32)
        y = cw_ref[3:4, :] * buf_ref[8:8 + tr, :]
        for j in range(CONV_K - 1):
            y = y + cw_ref[j:j + 1, :] * buf_ref[5 + j:5 + j + tr, :]
        buf_ref[0:8, :] = buf_ref[tr:tr + 8, :]
        dst_ref[...] = _silu(y)

    conv(uq_ref, cwq_ref, bq_ref, qs_ref)
    conv(uk_ref, cwk_ref, bk_ref, ks_ref)
    conv(uv_ref, cwv_ref, bv_ref, vs_ref)

    sm = sm_ref[...]
    beta_all = _sigmoid(sm)
    g_all = crow_ref[1:2, :] * _softplus(sm + crow_ref[0:1, :])
    smt = smt_ref[...]
    g_rows = ccol_ref[1, :, 0:1] * _softplus(smt + ccol_ref[0, :, 0:1])

    rowi = lax.broadcasted_iota(jnp.int32, (c, c), 0)
    coli = lax.broadcasted_iota(jnp.int32, (c, c), 1)
    incl = rowi >= coli
    strict = rowi > coli
    tri_l = jnp.where(incl, 1.0, 0.0).astype(BF16)
    tri_u = jnp.where(rowi <= coli, 1.0, 0.0).astype(BF16)
    ng = ng_ref[...]
    n_chunks = tr // c
    items = [(ci, hh) for ci in range(n_chunks) for hh in range(g_heads)]

    gc_cols, gc_rowsl = [], []
    for ci in range(n_chunks):
        sl = slice(ci * c, (ci + 1) * c)
        gh, gm, gl = _split3(g_all[sl, :])
        gc_cols.append(_dot(tri_l, gh) + _dot(tri_l, gm) + _dot(tri_l, gl))
        rh, rm, rl = _split3(g_rows[:, sl])
        gc_rowsl.append(_dot(rh, tri_u) + _dot(rm, tri_u) + _dot(rl, tri_u))

    qn, kn, vv, beta, gcc, gamma, qk, xm = {}, {}, {}, {}, {}, {}, {}, {}
    for it in items:
        ci, hh = it
        sl = slice(ci * c, (ci + 1) * c)
        hs = slice(hh * HEAD_DIM, (hh + 1) * HEAD_DIM)
        q = qs_ref[sl, hs]
        k = ks_ref[sl, hs]
        vv[it] = vs_ref[sl, hs]
        q = q * lax.rsqrt(jnp.sum(q * q, axis=-1, keepdims=True) + EPS) * (HEAD_DIM ** -0.5)
        k = k * lax.rsqrt(jnp.sum(k * k, axis=-1, keepdims=True) + EPS)
        qn[it], kn[it] = q, k
        beta[it] = beta_all[sl, hh:hh + 1]
        gcc[it] = gc_cols[ci][:, g_heads + hh:g_heads + hh + 1]
        gc_r = gc_rowsl[ci][g_heads + hh:g_heads + hh + 1, :]
        gamma[it] = jnp.exp(jnp.where(incl, gcc[it] - gc_r, -jnp.inf))
        kb = k.astype(BF16)
        qkk = _dot_nt(jnp.concatenate([q.astype(BF16), kb], axis=0), kb)
        qk[it] = qkk[:c]
        xm[it] = -jnp.where(strict, beta[it] * qkk[c:] * gamma[it], 0.0)

    n_lvls = int(np.log2(c)) - 3
    same8 = (rowi // 8) == (coli // 8)
    nm = {}
    for it in items:
        x8 = jnp.where(same8, xm[it], 0.0)
        xb = x8.astype(BF16)
        p1 = _dot(xb, xb)
        r = _dot(jnp.concatenate([x8, p1], axis=0).astype(BF16), p1.astype(BF16))
        n3 = x8 + p1 + r[:c]
        p2 = r[c:]
        nm[it] = n3 + p2 + _dot(n3.astype(BF16), p2.astype(BF16))
    for lvl in range(n_lvls):
        inner = (rowi // (8 << lvl)) == (coli // (8 << lvl))
        outer = (rowi // (16 << lvl)) == (coli // (16 << lvl))
        emask = outer & jnp.logical_not(inner)
        for it in items:
            e = jnp.where(emask, -xm[it], 0.0)
            m1 = e + _dot(nm[it].astype(BF16), e.astype(BF16))
            ded = m1 + _dot(m1.astype(BF16), nm[it].astype(BF16))
            nm[it] = nm[it] - ded

    q_eff, o_0, d_mat, c_mat, g_last = {}, {}, {}, {}, {}
    for it in items:
        e_gc = jnp.exp(gcc[it])
        rhs = jnp.concatenate([kn[it] * (beta[it] * e_gc), vv[it] * beta[it]], axis=1)
        solb = (rhs + _dot(nm[it].astype(BF16), rhs.astype(BF16))).astype(BF16)
        attn = (qk[it] * gamma[it]).astype(BF16)
        aw = _dot(attn, solb)
        gc_last = gcc[it][c - 1:c, :]
        k_dec = (kn[it] * jnp.exp(gc_last - gcc[it])).astype(BF16)
        kw = _dot_tn(k_dec, solb)
        q_eff[it] = (qn[it] * e_gc - aw[:, :HEAD_DIM]).astype(BF16)
        o_0[it] = aw[:, HEAD_DIM:]
        d_mat[it] = kw[:, :HEAD_DIM].astype(BF16)
        c_mat[it] = kw[:, HEAD_DIM:]
        g_last[it] = jnp.exp(gc_last)

    states = [state_ref[hh] for hh in range(g_heads)]
    for it in items:
        ci, hh = it
        sl = slice(ci * c, (ci + 1) * c)
        hs = slice(hh * HEAD_DIM, (hh + 1) * HEAD_DIM)
        st = states[hh]
        stb = st.astype(BF16)
        o = _dot(q_eff[it], stb) + o_0[it]
        states[hh] = st * g_last[it] + (c_mat[it] - _dot(d_mat[it], stb))
        o = _rms_rows(o) * ng * _silu(z_ref[sl, hs].astype(F32))
        o_ref[sl, hs] = o.astype(o_ref.dtype)
    for hh in range(g_heads):
        state_ref[hh] = states[hh]


def deltanet_mix(p, conv_w8, small, small_t, crow, ccol, norm_g, batch, seq, n_heads,
                 q_blk, k_blk, v_blk, z_blk, tr=512):
    t = p.shape[0]
    tr = min(tr, seq)
    ns = seq // tr
    ngroups = n_heads // DN_GROUP
    gw = DN_GROUP * HEAD_DIM
    kern = functools.partial(_dn_kernel, tr=tr)
    row = lambda b, g, s: b * ns + s
    return pl.pallas_call(
        kern,
        out_shape=jax.ShapeDtypeStruct((t, n_heads * HEAD_DIM), BF16),
        grid=(batch, ngroups, ns),
        in_specs=[pl.BlockSpec((tr, gw), lambda b, g, s: (row(b, g, s), q_blk + g)),
                  pl.BlockSpec((tr, gw), lambda b, g, s: (row(b, g, s), k_blk + g)),
                  pl.BlockSpec((tr, gw), lambda b, g, s: (row(b, g, s), v_blk + g)),
                  pl.BlockSpec((tr, gw), lambda b, g, s: (row(b, g, s), z_blk + g)),
                  pl.BlockSpec((8, gw), lambda b, g, s: (0, q_blk + g)),
                  pl.BlockSpec((8, gw), lambda b, g, s: (0, k_blk + g)),
                  pl.BlockSpec((8, gw), lambda b, g, s: (0, v_blk + g)),
                  pl.BlockSpec((tr, LANES), lambda b, g, s: (row(b, g, s), g)),
                  pl.BlockSpec((None, None, 8, tr), lambda b, g, s: (b, g, 0, s)),
                  pl.BlockSpec((None, 8, LANES), lambda b, g, s: (g, 0, 0)),
                  pl.BlockSpec((None, 2, 8, LANES), lambda b, g, s: (g, 0, 0, 0)),
                  pl.BlockSpec((1, HEAD_DIM), lambda b, g, s: (0, 0))],
        out_specs=pl.BlockSpec((tr, gw), lambda b, g, s: (row(b, g, s), g)),
        scratch_shapes=[pltpu.VMEM((DN_GROUP, HEAD_DIM, HEAD_DIM), F32),
                        pltpu.VMEM((tr + 8, gw), F32), pltpu.VMEM((tr + 8, gw), F32),
                        pltpu.VMEM((tr + 8, gw), F32),
                        pltpu.VMEM((tr, gw), F32), pltpu.VMEM((tr, gw), F32), pltpu.VMEM((tr, gw), F32)],
        compiler_params=_cparams(3), name="deltanet",
    )(p, p, p, p, conv_w8, conv_w8, conv_w8, small, small_t, crow, ccol, norm_g)


def _norm_router_kernel(x_ref, g_ref, sc_ref, sh_ref, wr_ref, br_ref, h_ref, lg_ref):
    x = x_ref[...]
    h = _rms_rows(x) * g_ref[...] * (1.0 + sc_ref[0]) + sh_ref[0]
    half = h.shape[1] // 2
    h_ref[...] = _bf16_bits_lo(h[:, :half]) | _bf16_bits_hi(h[:, half:])
    w = wr_ref[...]
    w_hi = w.astype(BF16)
    w_lo = (w - w_hi.astype(F32)).astype(BF16)
    h_hi = h.astype(BF16)
    h_lo = (h - h_hi.astype(F32)).astype(BF16)
    lg_ref[...] = _dot(h_hi, w_hi) + _dot(h_lo, w_hi) + _dot(h_hi, w_lo) + br_ref[...]


def norm_router(x2d, g, sc, sh, w_router_pad, b_router_pad, seq, tm=256):
    t, d = x2d.shape
    tm = min(tm, seq)
    nb = seq // tm
    return pl.pallas_call(
        _norm_router_kernel,
        out_shape=(jax.ShapeDtypeStruct((t, d // 2), U32), jax.ShapeDtypeStruct((t, LANES), F32)),
        grid=(t // tm,),
        in_specs=[pl.BlockSpec((tm, d), lambda i: (i, 0)),
                  pl.BlockSpec((1, d), lambda i: (0, 0)),
                  pl.BlockSpec((1, 1, d), lambda i: (i // nb, 0, 0)),
                  pl.BlockSpec((1, 1, d), lambda i: (i // nb, 0, 0)),
                  pl.BlockSpec((d, LANES), lambda i: (0, 0)),
                  pl.BlockSpec((1, LANES), lambda i: (0, 0))],
        out_specs=(pl.BlockSpec((tm, d // 2), lambda i: (i, 0)),
                   pl.BlockSpec((tm, LANES), lambda i: (i, 0))),
        compiler_params=_cparams(1), name="norm_router",
    )(x2d, g.reshape(1, d), sc, sh, w_router_pad, b_router_pad)


def _moe_kernel(blk_e_ref, first_ref, nxt_ref, nused_ref, tok_ref, tokn_ref, h_hbm, wgu_hbm, wd_hbm,
                bgu_ref, bd_ref, y_ref,
                stg_gu, stg_d, wgu_b, wd_b, perm_ref, xbuf0, xbuf1, wsem, gsem, *, tb, layer):
    i = pl.program_id(0)
    nused = nused_ref[0]
    d, f2 = stg_gu.shape
    f = f2 // 2
    xbufs = (xbuf0, xbuf1)

    def weight_copies(e):
        return (pltpu.make_async_copy(wgu_hbm.at[layer, e], stg_gu, wsem.at[0]),
                pltpu.make_async_copy(wd_hbm.at[layer, e], stg_d, wsem.at[1]))

    def row_copy(toks, r, s):
        return pltpu.make_async_copy(h_hbm.at[pl.ds(toks[0, r], 1), :],
                                     xbufs[s].at[pl.ds(r, 1), :], gsem.at[s])

    def wait_gather(s):
        pltpu.make_async_copy(h_hbm.at[pl.ds(0, tb), :], xbufs[s], gsem.at[s]).wait()

    @pl.when(i == 0)
    def _():
        for cp in weight_copies(blk_e_ref[0]):
            cp.start()

        def issue(r, carry):
            row_copy(tok_ref, r, 0).start()
            return carry
        lax.fori_loop(0, tb, issue, 0)
        rr = lax.broadcasted_iota(jnp.int32, (f2, f), 0)
        cc = lax.broadcasted_iota(jnp.int32, (f2, f), 1)
        perm_ref[...] = jnp.where(rr == 2 * cc, 1.0, 0.0).astype(BF16)

    @pl.when((i < nused) & (first_ref[i] == 1))
    def _():
        for cp in weight_copies(0):
            cp.wait()
        rows = min(256, f)

        def cast_gu(r, carry):
            r0 = pl.multiple_of(r * rows, rows)
            wgu_b[pl.ds(r0, rows), :] = stg_gu[pl.ds(r0, rows), :].astype(BF16)
            return carry
        lax.fori_loop(0, d // rows, cast_gu, 0)

        def cast_d(r, carry):
            r0 = pl.multiple_of(r * rows, rows)
            wd_b[pl.ds(r0, rows), :] = stg_d[pl.ds(r0, rows), :].astype(BF16)
            return carry
        lax.fori_loop(0, f // rows, cast_d, 0)

        @pl.when(nxt_ref[i] >= 0)
        def _():
            for cp in weight_copies(nxt_ref[i]):
                cp.start()

    def compute(s):
        wait_gather(s)
        xw = xbufs[s][...]
        x = jnp.concatenate([pltpu.bitcast(xw << 16, F32).astype(BF16),
                             pltpu.bitcast(xw & jnp.uint32(HI16), F32).astype(BF16)], axis=1)
        for r in range(tb):
            row_copy(tokn_ref, r, 1 - s).start()
        gu = _dot(x, wgu_b[...]) + bgu_ref[...]
        gate = jnp.minimum(gu, SWIGLU_LIMIT)
        sg = gate * _sigmoid(SWIGLU_ALPHA * gate)
        lin1 = jnp.clip(gu, -SWIGLU_LIMIT, SWIGLU_LIMIT) + 1.0
        act_il = sg * pltpu.roll(lin1, f2 - 1, 1)
        act = _dot(act_il.astype(BF16), perm_ref[...])
        y = _dot(act.astype(BF16), wd_b[...]) + bd_ref[...]
        y_ref[...] = _bf16_bits_lo(y[:, :d // 2]) | _bf16_bits_hi(y[:, d // 2:])

    for s in range(2):
        @pl.when((i < nused) & (i % 2 == s))
        def _(s=s):
            compute(s)

        @pl.when((i == nused) & (i % 2 == s))
        def _(s=s):
            wait_gather(s)

    @pl.when(i >= nused)
    def _():
        y_ref[...] = jnp.zeros_like(y_ref)


def moe_experts(h2, w_gu, w_down, b_gu_l, b_down_l, layer, blk_e, first, nxt, n_used, row_tok, tb):
    d = w_gu.shape[2]
    n_blk = row_tok.shape[0]
    f2 = w_gu.shape[3]
    f = f2 // 2
    kern = functools.partial(_moe_kernel, tb=tb, layer=layer)
    grid_spec = pltpu.PrefetchScalarGridSpec(
        num_scalar_prefetch=4,
        grid=(n_blk,),
        in_specs=[pl.BlockSpec((None, 1, tb), lambda i, *_: (i, 0, 0), memory_space=pltpu.SMEM),
                  pl.BlockSpec((None, 1, tb), lambda i, *_: (jnp.minimum(i + 1, n_blk - 1), 0, 0),
                               memory_space=pltpu.SMEM),
                  pl.BlockSpec(memory_space=pl.ANY),
                  pl.BlockSpec(memory_space=pl.ANY),
                  pl.BlockSpec(memory_space=pl.ANY),
                  pl.BlockSpec((None, 1, f2), lambda i, be, *_: (be[i], 0, 0)),
                  pl.BlockSpec((None, 1, d), lambda i, be, *_: (be[i], 0, 0))],
        out_specs=pl.BlockSpec((tb, d // 2), lambda i, *_: (i, 0)),
        scratch_shapes=[pltpu.VMEM((d, f2), F32), pltpu.VMEM((f, d), F32),
                        pltpu.VMEM((d, f2), BF16), pltpu.VMEM((f, d), BF16),
                        pltpu.VMEM((f2, f), BF16), pltpu.VMEM((tb, d // 2), U32),
                        pltpu.VMEM((tb, d // 2), U32),
                        pltpu.SemaphoreType.DMA((2,)), pltpu.SemaphoreType.DMA((2,))])
    return pl.pallas_call(
        kern,
        out_shape=jax.ShapeDtypeStruct((n_blk * tb, d // 2), U32),
        grid_spec=grid_spec,
        compiler_params=_cparams(1), name="moe_experts",
    )(blk_e, first, nxt, n_used, row_tok, row_tok, h2, w_gu, w_down, b_gu_l, b_down_l)


def _combine_kernel(pos_ref, posn_ref, y_hbm, x_ref, tw_ref, g2_ref, ng_ref, sc_ref, sh_ref, *rest,
                    tm, final, n_steps):
    if final:
        out_ref, ybuf0, ybuf1, sem = rest
    else:
        xo_ref, h_ref, ybuf0, ybuf1, sem = rest
    i = pl.program_id(0)
    ybufs = (ybuf0, ybuf1)

    def start_gather(p_ref, s):
        def issue(r, carry):
            for k in range(TOP_K):
                pltpu.make_async_copy(y_hbm.at[pl.ds(p_ref[0, r * TOP_K + k], 1), :],
                                      ybufs[s].at[k, pl.ds(r, 1), :], sem.at[s]).start()
            return carry
        lax.fori_loop(0, tm, issue, 0)

    def wait_gather(s):
        for k in range(TOP_K):
            pltpu.make_async_copy(y_hbm.at[pl.ds(0, tm), :], ybufs[s].at[k], sem.at[s]).wait()

    def lo(w):
        return pltpu.bitcast(w << 16, F32)

    def hi(w):
        return pltpu.bitcast(w & jnp.uint32(HI16), F32)

    def finish(s):
        tw = tw_ref[...]
        acc_lo = acc_hi = None
        for k in range(TOP_K):
            w = ybufs[s][k]
            wk = tw[:, k:k + 1]
            acc_lo = wk * lo(w) if acc_lo is None else acc_lo + wk * lo(w)
            acc_hi = wk * hi(w) if acc_hi is None else acc_hi + wk * hi(w)
        x = x_ref[...] + g2_ref[0] * jnp.concatenate([acc_lo, acc_hi], axis=1)
        if final:
            out_ref[...] = _rms_rows(x) * ng_ref[...]
        else:
            xo_ref[...] = x
            h = _rms_rows(x) * ng_ref[...] * (1.0 + sc_ref[0]) + sh_ref[0]
            h_ref[...] = h.astype(h_ref.dtype)

    @pl.when(i == 0)
    def _():
        start_gather(pos_ref, 0)

    for s in range(2):
        @pl.when(i % 2 == s)
        def _(s=s):
            @pl.when(i + 1 < n_steps)
            def _():
                start_gather(posn_ref, 1 - s)
            wait_gather(s)
            finish(s)


def moe_combine(y, pos, top_w_pad, x2d, gate2, norm_g, sc, sh, seq, final, tm=128):
    t, d = x2d.shape
    tm = min(tm, seq)
    nb = seq // tm
    n_steps = t // tm
    kern = functools.partial(_combine_kernel, tm=tm, final=final, n_steps=n_steps)
    if final:
        out_shape = jax.ShapeDtypeStruct((t, d), F32)
        out_specs = pl.BlockSpec((tm, d), lambda i: (i, 0))
    else:
        out_shape = (jax.ShapeDtypeStruct((t, d), F32), jax.ShapeDtypeStruct((t, d), BF16))
        out_specs = (pl.BlockSpec((tm, d), lambda i: (i, 0)), pl.BlockSpec((tm, d), lambda i: (i, 0)))
    pos3 = pos.reshape(n_steps, 1, tm * TOP_K)
    return pl.pallas_call(
        kern,
        out_shape=out_shape,
        grid=(n_steps,),
        in_specs=[pl.BlockSpec((None, 1, tm * TOP_K), lambda i: (i, 0, 0), memory_space=pltpu.SMEM),
                  pl.BlockSpec((None, 1, tm * TOP_K), lambda i: (jnp.minimum(i + 1, n_steps - 1), 0, 0),
                               memory_space=pltpu.SMEM),
                  pl.BlockSpec(memory_space=pl.ANY),
                  pl.BlockSpec((tm, d), lambda i: (i, 0)),
                  pl.BlockSpec((tm, LANES), lambda i: (i, 0)),
                  pl.BlockSpec((1, 1, d), lambda i: (i // nb, 0, 0)),
                  pl.BlockSpec((1, d), lambda i: (0, 0)),
                  pl.BlockSpec((1, 1, d), lambda i: (i // nb, 0, 0)),
                  pl.BlockSpec((1, 1, d), lambda i: (i // nb, 0, 0))],
        out_specs=out_specs,
        scratch_shapes=[pltpu.VMEM((TOP_K, tm, d // 2), U32), pltpu.VMEM((TOP_K, tm, d // 2), U32),
                        pltpu.SemaphoreType.DMA((2,))],
        compiler_params=_cparams(1), name="moe_combine",
    )(pos3, pos3, y, x2d, top_w_pad, gate2, norm_g.reshape(1, d), sc, sh)


def _routing_tables(logits, n_experts, tb):
    t = logits.shape[0]
    top_v, top_e = lax.top_k(logits[:, :n_experts], TOP_K)
    top_w = jax.nn.softmax(top_v, axis=-1)
    n_a = t * TOP_K
    eids = jnp.arange(n_experts, dtype=jnp.int32)
    sel = jnp.any(top_e[:, :, None] == eids[None, None, :], axis=1).astype(jnp.int32)
    csum = jnp.cumsum(sel, axis=0)
    counts = csum[-1]
    rank = jnp.take_along_axis(csum - sel, top_e, axis=1)
    padded = (counts + tb - 1) // tb * tb
    pend = jnp.cumsum(padded)
    pstart = pend - padded
    pos = (pstart[top_e] + rank).astype(jnp.int32).reshape(n_a)
    n_rows = (n_a + n_experts * (tb - 1) + tb - 1) // tb * tb + tb
    n_blk = n_rows // tb
    t_a = jnp.repeat(jnp.arange(t, dtype=jnp.int32), TOP_K)
    row_tok = jnp.zeros((n_rows,), jnp.int32).at[pos].set(t_a)
    blk_start = jnp.arange(n_blk, dtype=jnp.int32) * tb
    blk_e = jnp.minimum(jnp.sum((pend[None, :] <= blk_start[:, None]).astype(jnp.int32), axis=1),
                        n_experts - 1).astype(jnp.int32)
    n_used = (pend[-1] // tb).astype(jnp.int32)
    used = jnp.arange(n_blk) < n_used
    first = (used & (blk_start == pstart[blk_e])).astype(jnp.int32)
    has = counts > 0
    cand = jnp.where(has, eids, n_experts)
    suffix_min = lax.cummin(cand, axis=0, reverse=True)
    nxt_e = jnp.concatenate([suffix_min[1:], jnp.full((1,), n_experts, jnp.int32)])
    nxt_e = jnp.where(nxt_e >= n_experts, -1, nxt_e).astype(jnp.int32)
    nxt = nxt_e[blk_e]
    top_w_pad = jnp.pad(top_w, ((0, 0), (0, LANES - TOP_K)))
    return pos, top_w_pad, row_tok.reshape(n_blk, 1, tb), blk_e, first, nxt, n_used.reshape(1)


def _rope_tables(seq):
    half = HEAD_DIM // 2
    inv = ROPE_BASE ** (-jnp.arange(half, dtype=F32) / half)
    ang = jnp.arange(seq, dtype=F32)[:, None] * inv[None, :]
    cos, sin = jnp.cos(ang), jnp.sin(ang)
    return jnp.concatenate([cos, cos], axis=-1), jnp.concatenate([-sin, sin], axis=-1)


def kernel(x, c, norm1_g, w_ada, b_ada, w_in, dn_conv_w, dn_a_log, dn_dt_bias, dn_norm_g, fox_f_bias,
           w_out, norm2_g, w_router, b_router, w_gu, b_gu, w_down, b_down, final_norm_g):
    batch, seq, d = x.shape
    depth = w_ada.shape[0]
    t = batch * seq
    dn_heads = dn_a_log.shape[1]
    fox_heads = fox_f_bias.shape[1]
    ret_heads = d // HEAD_DIM - dn_heads - fox_heads
    dn_w, ret_w, fox_w = dn_heads * HEAD_DIM, ret_heads * HEAD_DIM, fox_heads * HEAD_DIM
    n_experts = w_router.shape[2]
    n_groups = dn_heads // DN_GROUP
    gw = DN_GROUP * HEAD_DIM
    moe_tb = 256
    tn = 512

    widths = (3 * dn_w, dn_w, dn_heads, dn_heads, ret_w, ret_w, ret_w, ret_w,
              fox_w, fox_w, fox_w, fox_w, fox_heads)
    cuts = np.concatenate([[0], np.cumsum(widths)])
    o_b, o_a, o_r, o_ff = int(cuts[2]), int(cuts[3]), int(cuts[4]), int(cuts[12])
    n_main = o_b + (o_ff - o_r)
    small_src, small_dst = [], []
    for g in range(n_groups):
        for hh in range(DN_GROUP):
            small_src += [g * DN_GROUP + hh, dn_heads + g * DN_GROUP + hh]
            small_dst += [g * LANES + hh, g * LANES + DN_GROUP + hh]
    for hh in range(fox_heads):
        small_src.append(2 * dn_heads + hh)
        small_dst.append(n_groups * LANES + hh)
    small_src = np.asarray(small_src, np.int32)
    small_dst = np.asarray(small_dst, np.int32)
    w_in_t = jnp.transpose(w_in, (2, 0, 1))
    w_main_t = w_in_prep(w_in_t, o_b, o_r - o_b, n_main)
    w_sc = jnp.concatenate([w_in_t[o_b:o_r], w_in_t[o_ff:]], axis=0)
    w_small_t = jnp.zeros((depth, (n_groups + 1) * LANES, d), BF16).at[:, small_dst].set(
        jnp.transpose(w_sc, (1, 0, 2))[:, small_src].astype(BF16))

    x2d = x.reshape(t, d)
    c_pad = jnp.zeros((8, d), BF16).at[:batch].set(c.astype(BF16))
    mod = ada_mod(c_pad, w_ada, b_ada)[:, :batch, :]
    mod = mod.reshape(depth, batch, 6, 1, d)
    cos2, sin2 = _rope_tables(seq)
    lg_tab = jnp.log(1.0 - 2.0 ** (-5.0 - jnp.arange(ret_heads, dtype=F32)))
    lg_tab = jnp.broadcast_to(lg_tab[:, None, None], (ret_heads, 8, LANES))

    dn_q_blk, dn_k_blk, dn_v_blk, dn_z_blk = 0, dn_w // gw, 2 * dn_w // gw, 3 * dn_w // gw
    r0 = o_b // HEAD_DIM
    r_q_blk, r_k_blk, r_v_blk, r_g_blk = r0, r0 + ret_heads, r0 + 2 * ret_heads, r0 + 3 * ret_heads
    f0 = r0 + 4 * ret_heads
    f_q_blk, f_k_blk, f_v_blk, f_z_blk = f0, f0 + fox_heads, f0 + 2 * fox_heads, f0 + 3 * fox_heads

    h = norm_mod(x2d, norm1_g[0], mod[0, :, 1], mod[0, :, 0], seq)
    out = None
    for l in range(depth):
        sh1, sc1, g1, sh2, sc2, g2 = (mod[l, :, i] for i in range(6))
        p = matmul_nt(h, w_main_t, l, BF16, "in_proj")
        small = matmul_nt(h, w_small_t, l, F32, "small_proj")

        conv_w8 = jnp.zeros((8, 3 * dn_w), F32).at[:CONV_K].set(dn_conv_w[l])
        small_t = small[:, :n_groups * LANES].reshape(batch, seq, n_groups, LANES)[..., :8]
        small_t = small_t.transpose(0, 2, 3, 1)
        dtb = dn_dt_bias[l].reshape(n_groups, DN_GROUP)
        nega = -jnp.exp(dn_a_log[l]).reshape(n_groups, DN_GROUP)
        zg = jnp.zeros((n_groups, DN_GROUP), F32)
        pad = jnp.zeros((n_groups, LANES - 2 * DN_GROUP), F32)
        crow = jnp.stack([jnp.concatenate([zg, dtb, pad], axis=1),
                          jnp.concatenate([zg, nega, pad], axis=1)], axis=1)
        crow = jnp.concatenate([crow, jnp.zeros((n_groups, 6, LANES), F32)], axis=1)
        ccol = jnp.stack([jnp.concatenate([zg, dtb], axis=1), jnp.concatenate([zg, nega], axis=1)], axis=1)
        ccol = jnp.broadcast_to(ccol[..., None], (n_groups, 2, 8, LANES))
        o_dn = deltanet_mix(p, conv_w8, small, small_t, crow, ccol, dn_norm_g[l].reshape(1, HEAD_DIM),
                            batch, seq, dn_heads, dn_q_blk, dn_k_blk, dn_v_blk, dn_z_blk)
        o_ret = retention_mix(p, cos2, sin2, lg_tab, batch, seq, ret_heads,
                              r_q_blk, r_k_blk, r_v_blk, r_g_blk)
        fb = jnp.zeros((1, LANES), F32).at[0, :fox_heads].set(fox_f_bias[l])
        f_cum = fox_prefix(small, fb, batch, seq, n_groups)
        o_fox = fox_attention(p, f_cum, batch, seq, fox_heads, f_q_blk, f_k_blk, f_v_blk, f_z_blk)
        x2d = out_proj_resid(o_dn, o_ret, o_fox, w_out, l, x2d, g1, seq)

        wr_pad = jnp.zeros((d, LANES), F32).at[:, :n_experts].set(w_router[l])
        br_pad = jnp.zeros((1, LANES), F32).at[0, :n_experts].set(b_router[l])
        h2, logits = norm_router(x2d, norm2_g[l], sc2, sh2, wr_pad, br_pad, seq)
        pos, top_w_pad, row_tok, blk_e, first, nxt, n_used = _routing_tables(logits, n_experts, moe_tb)
        y = moe_experts(h2, w_gu, w_down, b_gu[l][:, None, :], b_down[l][:, None, :], l,
                        blk_e, first, nxt, n_used, row_tok, moe_tb)
        if l + 1 < depth:
            nsh1, nsc1 = mod[l + 1, :, 0], mod[l + 1, :, 1]
            x2d, h = moe_combine(y, pos, top_w_pad, x2d, g2, norm1_g[l + 1], nsc1, nsh1, seq, final=False)
        else:
            out = moe_combine(y, pos, top_w_pad, x2d, g2, final_norm_g, sc2, sh2, seq, final=True)
    return out.reshape(batch, seq, d)
```

```python
import functools

import numpy as np
import jax
import jax.numpy as jnp
from jax import lax
from jax.experimental import pallas as pl
from jax.experimental.pallas import tpu as pltpu

F32 = jnp.float32
BF16 = jnp.bfloat16
U32 = jnp.uint32

HEAD_DIM = 128
CONV_K = 4
DN_CHUNK = 128
RET_CHUNK = 128
ROPE_BASE = 10000.0
TOP_K = 4
SWIGLU_LIMIT = 7.0
SWIGLU_ALPHA = 1.702
EPS = 1e-6
DN_GROUP = 4
LANES = 128
VMEM_LIMIT = 56 * 1024 * 1024
LOG2E = 1.4426950408889634
HI16 = 0xFFFF0000


def _cparams(n_axes):
    return pltpu.CompilerParams(dimension_semantics=("arbitrary",) * n_axes,
                                vmem_limit_bytes=VMEM_LIMIT)


def _dot(a, b):
    return jnp.dot(a, b, preferred_element_type=F32)


def _dot_nt(a, b):
    return lax.dot_general(a, b, (((1,), (1,)), ((), ())), preferred_element_type=F32)


def _dot_tn(a, b):
    return lax.dot_general(a, b, (((0,), (0,)), ((), ())), preferred_element_type=F32)


def _split3(x):
    h = x.astype(BF16)
    r = x - h.astype(F32)
    m = r.astype(BF16)
    l = (r - m.astype(F32)).astype(BF16)
    return h, m, l


def _sigmoid(x):
    return 1.0 / (1.0 + jnp.exp(-x))


def _silu(x):
    return x * _sigmoid(x)


def _softplus(x):
    return jnp.maximum(x, 0.0) + jnp.log1p(jnp.exp(-jnp.abs(x)))


def _log_sigmoid(x):
    return jnp.minimum(x, 0.0) - jnp.log1p(jnp.exp(-jnp.abs(x)))


def _rms_rows(x):
    return x * lax.rsqrt(jnp.mean(x * x, axis=-1, keepdims=True) + EPS)


def _bf16_bits_lo(x):
    return pltpu.bitcast(x.astype(BF16).astype(F32), U32) >> 16


def _bf16_bits_hi(x):
    return pltpu.bitcast(x.astype(BF16).astype(F32), U32) & jnp.uint32(HI16)


def _ada_kernel(c_ref, w_ref, b_ref, o_ref):
    o_ref[...] = _dot(c_ref[...], w_ref[...].astype(BF16)) + b_ref[...]


def ada_mod(c_pad, w_ada, b_ada, tn=512):
    depth, d, n = w_ada.shape
    return pl.pallas_call(
        _ada_kernel,
        out_shape=jax.ShapeDtypeStruct((depth, 8, n), F32),
        grid=(depth, n // tn),
        in_specs=[pl.BlockSpec((8, d), lambda l, j: (0, 0)),
                  pl.BlockSpec((None, d, tn), lambda l, j: (l, 0, j)),
                  pl.BlockSpec((None, 1, tn), lambda l, j: (l, 0, j))],
        out_specs=pl.BlockSpec((None, 8, tn), lambda l, j: (l, 0, j)),
        compiler_params=_cparams(2), name="ada_mod",
    )(c_pad, w_ada, b_ada.reshape(depth, 1, n))


def _norm_mod_kernel(x_ref, g_ref, sc_ref, sh_ref, h_ref):
    x = x_ref[...]
    h = _rms_rows(x) * g_ref[...] * (1.0 + sc_ref[0]) + sh_ref[0]
    h_ref[...] = h.astype(h_ref.dtype)


def norm_mod(x2d, g, sc, sh, seq, tm=256):
    t, d = x2d.shape
    tm = min(tm, seq)
    nb = seq // tm
    return pl.pallas_call(
        _norm_mod_kernel,
        out_shape=jax.ShapeDtypeStruct((t, d), BF16),
        grid=(t // tm,),
        in_specs=[pl.BlockSpec((tm, d), lambda i: (i, 0)),
                  pl.BlockSpec((1, d), lambda i: (0, 0)),
                  pl.BlockSpec((1, 1, d), lambda i: (i // nb, 0, 0)),
                  pl.BlockSpec((1, 1, d), lambda i: (i // nb, 0, 0))],
        out_specs=pl.BlockSpec((tm, d), lambda i: (i, 0)),
        compiler_params=_cparams(1), name="norm_mod",
    )(x2d, g.reshape(1, d), sc, sh)


def _w_prep_kernel(w_ref, o_ref):
    for l in range(o_ref.shape[0]):
        o_ref[l] = w_ref[:, l, :].astype(BF16)


def w_in_prep(w_in_t, gap_start, gap_len, n_rows, rows=256):
    n, depth, k = w_in_t.shape
    rows = min(rows, gap_start)
    assert gap_start % rows == 0 and n_rows % rows == 0 and gap_len % 8 == 0

    def in_idx(j):
        r = j * rows
        return (r + jnp.where(r >= gap_start, gap_len, 0), 0, 0)
    return pl.pallas_call(
        _w_prep_kernel,
        out_shape=jax.ShapeDtypeStruct((depth, n_rows, k), BF16),
        grid=(n_rows // rows,),
        in_specs=[pl.BlockSpec((pl.Element(rows), pl.Element(depth), pl.Element(k)), in_idx)],
        out_specs=pl.BlockSpec((depth, rows, k), lambda j: (0, j, 0)),
        compiler_params=_cparams(1), name="w_in_prep",
    )(w_in_t)


def _mm_nt_kernel(x_ref, w_ref, o_ref):
    o_ref[...] = _dot_nt(x_ref[...], w_ref[...]).astype(o_ref.dtype)


def matmul_nt(x, w_t, layer, out_dtype, name, tm=1024, tn=512):
    m, k = x.shape
    n = w_t.shape[1]
    tm = min(tm, m)
    tn = min(tn, n)
    return pl.pallas_call(
        _mm_nt_kernel,
        out_shape=jax.ShapeDtypeStruct((m, n), out_dtype),
        grid=(n // tn, m // tm),
        in_specs=[pl.BlockSpec((tm, k), lambda j, i: (i, 0)),
                  pl.BlockSpec((None, tn, k), lambda j, i: (layer, j, 0))],
        out_specs=pl.BlockSpec((tm, tn), lambda j, i: (i, j)),
        compiler_params=_cparams(2), name=name,
    )(x, w_t)


def _mm_resid_kernel(a_ref, b_ref, c_ref, w_ref, x_ref, g_ref, o_ref, wb_ref, *, splits):
    @pl.when(pl.program_id(1) == 0)
    def _():
        wb_ref[...] = w_ref[...].astype(BF16)
    k0, k1 = splits
    y = (_dot(a_ref[...], wb_ref[0:k0, :]) + _dot(b_ref[...], wb_ref[k0:k1, :])
         + _dot(c_ref[...], wb_ref[k1:, :]))
    o_ref[...] = x_ref[...] + g_ref[0] * y


def out_proj_resid(a, b, c, w_out, layer, x2d, gate, seq, tm=1024, tn=512):
    m = a.shape[0]
    ka, kb, kc = a.shape[1], b.shape[1], c.shape[1]
    k = ka + kb + kc
    n = w_out.shape[2]
    tm = min(tm, seq)
    nb = seq // tm
    kern = functools.partial(_mm_resid_kernel, splits=(ka, ka + kb))
    return pl.pallas_call(
        kern,
        out_shape=jax.ShapeDtypeStruct((m, n), F32),
        grid=(n // tn, m // tm),
        in_specs=[pl.BlockSpec((tm, ka), lambda j, i: (i, 0)),
                  pl.BlockSpec((tm, kb), lambda j, i: (i, 0)),
                  pl.BlockSpec((tm, kc), lambda j, i: (i, 0)),
                  pl.BlockSpec((None, k, tn), lambda j, i: (layer, 0, j)),
                  pl.BlockSpec((tm, tn), lambda j, i: (i, j)),
                  pl.BlockSpec((1, 1, tn), lambda j, i: (i // nb, 0, j))],
        out_specs=pl.BlockSpec((tm, tn), lambda j, i: (i, j)),
        scratch_shapes=[pltpu.VMEM((k, tn), BF16)],
        compiler_params=_cparams(2), name="out_proj",
    )(a, b, c, w_out, x2d, gate)


def _fprep_kernel(s_ref, b_ref, f_ref, carry_ref):
    @pl.when(pl.program_id(1) == 0)
    def _():
        carry_ref[...] = jnp.zeros_like(carry_ref)
    tq = s_ref.shape[0]
    lf = _log_sigmoid(s_ref[...] + b_ref[...])
    row = lax.broadcasted_iota(jnp.int32, (tq, tq), 0)
    col = lax.broadcasted_iota(jnp.int32, (tq, tq), 1)
    tri = jnp.where(row >= col, 1.0, 0.0).astype(BF16)
    h, m, l = _split3(lf)
    f = _dot(tri, h) + _dot(tri, m) + _dot(tri, l) + carry_ref[0:1, :]
    f_ref[...] = f
    carry_ref[...] = jnp.broadcast_to(f[tq - 1:tq, :], carry_ref.shape)


def fox_prefix(small, bias_row, batch, seq, col_block, tq=512):
    t = small.shape[0]
    tq = min(tq, seq)
    nq = seq // tq
    return pl.pallas_call(
        _fprep_kernel,
        out_shape=jax.ShapeDtypeStruct((t, LANES), F32),
        grid=(batch, nq),
        in_specs=[pl.BlockSpec((tq, LANES), lambda b, i: (b * nq + i, col_block)),
                  pl.BlockSpec((1, LANES), lambda b, i: (0, 0))],
        out_specs=pl.BlockSpec((tq, LANES), lambda b, i: (b * nq + i, 0)),
        scratch_shapes=[pltpu.VMEM((8, LANES), F32)],
        compiler_params=_cparams(2), name="fox_prefix",
    )(small, bias_row)


FOX_VT_ROWS = HEAD_DIM + 16


def _fox_kernel(q_ref, k_ref, v_ref, z_ref, f_ref, o_ref, ka_ref, vt_ref, m_ref, acc_ref, *, tq, seq):
    h = pl.program_id(1)
    qi = pl.program_id(2)
    lane = lax.broadcasted_iota(jnp.int32, (tq, LANES), 1)

    def f_column(rows):
        fb = f_ref[pl.ds(rows, tq), :]
        return jnp.sum(jnp.where(lane == h, fb, 0.0), axis=-1, keepdims=True) * LOG2E

    def aug_lanes(fcol, sign_first):
        fh, fm, fl = _split3(fcol)
        fh, fm, fl = fh.astype(F32), fm.astype(F32), fl.astype(F32)
        one = jnp.ones((tq, LANES), F32)
        zero = jnp.zeros((tq, LANES), F32)
        if sign_first:
            a = jnp.where(lane == 0, fh, jnp.where(lane == 1, fm, jnp.where(lane == 2, fl,
                          jnp.where(lane < 6, one, zero))))
        else:
            a = jnp.where(lane < 3, one, jnp.where(lane == 3, -fh, jnp.where(lane == 4, -fm,
                          jnp.where(lane == 5, -fl, zero))))
        return a.astype(BF16)

    @pl.when(qi == 0)
    def _():
        def build(j, carry):
            r0 = pl.multiple_of(j * tq, tq)
            ka_ref[pl.ds(r0, tq), 0:HEAD_DIM] = k_ref[pl.ds(r0, tq), :]
            ka_ref[pl.ds(r0, tq), HEAD_DIM:] = aug_lanes(f_column(r0), False)
            vt_ref[0:HEAD_DIM, pl.ds(r0, tq)] = v_ref[pl.ds(r0, tq), :].astype(F32).T.astype(BF16)
            vt_ref[HEAD_DIM:, pl.ds(r0, tq)] = jnp.ones((FOX_VT_ROWS - HEAD_DIM, tq), BF16)
            return carry
        lax.fori_loop(0, seq // tq, build, 0)

    q0 = pl.multiple_of(qi * tq, tq)
    qs = (q_ref[...].astype(F32) * (HEAD_DIM ** -0.5 * LOG2E)).astype(BF16)
    qa = jnp.concatenate([qs, aug_lanes(f_column(q0), True)], axis=1)

    m_ref[...] = jnp.full_like(m_ref, -1e30)
    acc_ref[...] = jnp.zeros_like(acc_ref)

    def block(k0, width, masked):
        st = _dot_nt(ka_ref[pl.ds(k0, width), :], qa)
        if masked:
            key = lax.broadcasted_iota(jnp.int32, (width, tq), 0)
            qry = lax.broadcasted_iota(jnp.int32, (width, tq), 1)
            st = jnp.where(qry >= key, st, -1e30)
        m_old = m_ref[...]
        m_new = jnp.maximum(m_old, jnp.max(st, axis=0, keepdims=True))
        p = jnp.exp2(st - m_new).astype(BF16)
        acc_ref[...] = (jnp.exp2(m_old - m_new) * acc_ref[...]
                        + _dot(vt_ref[:, pl.ds(k0, width)], p))
        m_ref[...] = m_new

    def wide(j, carry):
        block(pl.multiple_of(j * (2 * tq), 2 * tq), 2 * tq, False)
        return carry
    lax.fori_loop(0, qi // 2, wide, 0)

    @pl.when(qi % 2 == 1)
    def _():
        block(pl.multiple_of((qi - 1) * tq, tq), tq, False)

    block(q0, tq, True)
    acc = acc_ref[...]
    o = (acc[:HEAD_DIM] / acc[HEAD_DIM:HEAD_DIM + 1]).T
    o_ref[...] = (o * _sigmoid(z_ref[...].astype(F32))).astype(o_ref.dtype)


def fox_attention(p, f_cum, batch, seq, n_heads, q_blk, k_blk, v_blk, z_blk, tq=512):
    t = p.shape[0]
    tq = min(tq, seq)
    nq = seq // tq
    kern = functools.partial(_fox_kernel, tq=tq, seq=seq)
    return pl.pallas_call(
        kern,
        out_shape=jax.ShapeDtypeStruct((t, n_heads * HEAD_DIM), BF16),
        grid=(batch, n_heads, nq),
        in_specs=[pl.BlockSpec((tq, HEAD_DIM), lambda b, h, i: (b * nq + i, q_blk + h)),
                  pl.BlockSpec((seq, HEAD_DIM), lambda b, h, i: (b, k_blk + h)),
                  pl.BlockSpec((seq, HEAD_DIM), lambda b, h, i: (b, v_blk + h)),
                  pl.BlockSpec((tq, HEAD_DIM), lambda b, h, i: (b * nq + i, z_blk + h)),
                  pl.BlockSpec((seq, LANES), lambda b, h, i: (b, 0))],
        out_specs=pl.BlockSpec((tq, HEAD_DIM), lambda b, h, i: (b * nq + i, h)),
        scratch_shapes=[pltpu.VMEM((seq, 2 * HEAD_DIM), BF16), pltpu.VMEM((FOX_VT_ROWS, seq), BF16),
                        pltpu.VMEM((1, tq), F32), pltpu.VMEM((FOX_VT_ROWS, tq), F32)],
        compiler_params=_cparams(3), name="fox_attn",
    )(p, p, p, p, f_cum)


def _ret_kernel(q_ref, k_ref, v_ref, g_ref, cos_ref, sin_ref, lg_ref, o_ref, r_ref, *, tc):
    c = RET_CHUNK

    @pl.when(pl.program_id(2) == 0)
    def _():
        r_ref[...] = jnp.zeros_like(r_ref)

    lg = lg_ref[0:1, 0:1]
    rowi = lax.broadcasted_iota(jnp.int32, (c, c), 0)
    coli = lax.broadcasted_iota(jnp.int32, (c, c), 1)
    diff = (rowi - coli).astype(F32)
    decay = jnp.where(diff >= 0, jnp.exp(lg * jnp.maximum(diff, 0.0)), 0.0)
    pos = lax.broadcasted_iota(jnp.int32, (c, 1), 0).astype(F32)
    q_scale = jnp.exp(lg * (pos + 1.0))
    k_scale = jnp.exp(lg * (c - 1.0 - pos))
    chunk_decay = jnp.exp(lg * c)

    def rope(t, cos2, sin2):
        return t * cos2 + pltpu.roll(t, HEAD_DIM // 2, 1) * sin2

    for ci in range(tc // c):
        sl = slice(ci * c, (ci + 1) * c)
        cos2 = cos_ref[sl, :]
        sin2 = sin_ref[sl, :]
        q = rope(q_ref[sl, :].astype(F32), cos2, sin2)
        k = rope(k_ref[sl, :].astype(F32), cos2, sin2) * (HEAD_DIM ** -0.5)
        v = v_ref[sl, :]
        inner = _dot_nt(q.astype(BF16), k.astype(BF16)) * decay
        r_prev = r_ref[...]
        o = _dot(inner.astype(BF16), v) + _dot((q * q_scale).astype(BF16), r_prev.astype(BF16))
        r_ref[...] = r_prev * chunk_decay + _dot_tn((k * k_scale).astype(BF16), v)
        o = _rms_rows(o) * _silu(g_ref[sl, :].astype(F32))
        o_ref[sl, :] = o.astype(o_ref.dtype)


def retention_mix(p, cos2, sin2, lg_tab, batch, seq, n_heads, q_blk, k_blk, v_blk, g_blk, tc=512):
    t = p.shape[0]
    tc = min(tc, seq)
    ns = seq // tc
    kern = functools.partial(_ret_kernel, tc=tc)
    return pl.pallas_call(
        kern,
        out_shape=jax.ShapeDtypeStruct((t, n_heads * HEAD_DIM), BF16),
        grid=(batch, n_heads, ns),
        in_specs=[pl.BlockSpec((tc, HEAD_DIM), lambda b, h, s: (b * ns + s, q_blk + h)),
                  pl.BlockSpec((tc, HEAD_DIM), lambda b, h, s: (b * ns + s, k_blk + h)),
                  pl.BlockSpec((tc, HEAD_DIM), lambda b, h, s: (b * ns + s, v_blk + h)),
                  pl.BlockSpec((tc, HEAD_DIM), lambda b, h, s: (b * ns + s, g_blk + h)),
                  pl.BlockSpec((tc, HEAD_DIM), lambda b, h, s: (s, 0)),
                  pl.BlockSpec((tc, HEAD_DIM), lambda b, h, s: (s, 0)),
                  pl.BlockSpec((None, 8, LANES), lambda b, h, s: (h, 0, 0))],
        out_specs=pl.BlockSpec((tc, HEAD_DIM), lambda b, h, s: (b * ns + s, h)),
        scratch_shapes=[pltpu.VMEM((HEAD_DIM, HEAD_DIM), F32)],
        compiler_params=_cparams(3), name="retention",
    )(p, p, p, p, cos2, sin2, lg_tab)


def _dn_kernel(uq_ref, uk_ref, uv_ref, z_ref, cwq_ref, cwk_ref, cwv_ref, sm_ref, smt_ref,
               crow_ref, ccol_ref, ng_ref, o_ref,
               state_ref, bq_ref, bk_ref, bv_ref, qs_ref, ks_ref, vs_ref, *, tr):
    c = DN_CHUNK
    g_heads = DN_GROUP
    gw = g_heads * HEAD_DIM

    @pl.when(pl.program_id(2) == 0)
    def _():
        state_ref[...] = jnp.zeros_like(state_ref)
        bq_ref[0:8, :] = jnp.zeros((8, gw), F32)
        bk_ref[0:8, :] = jnp.zeros((8, gw), F32)
        bv_ref[0:8, :] = jnp.zeros((8, gw), F32)

    def conv(u_ref, cw_ref, buf_ref, dst_ref):
        buf_ref[8:8 + tr, :] = u_ref[...].astype(F32)
        y = cw_ref[3:4, :] * buf_ref[8:8 + tr, :]
        for j in range(CONV_K - 1):
            y = y + cw_ref[j:j + 1, :] * buf_ref[5 + j:5 + j + tr, :]
        buf_ref[0:8, :] = buf_ref[tr:tr + 8, :]
        dst_ref[...] = _silu(y)

    conv(uq_ref, cwq_ref, bq_ref, qs_ref)
    conv(uk_ref, cwk_ref, bk_ref, ks_ref)
    conv(uv_ref, cwv_ref, bv_ref, vs_ref)

    sm = sm_ref[...]
    beta_all = _sigmoid(sm)
    g_all = crow_ref[1:2, :] * _softplus(sm + crow_ref[0:1, :])
    smt = smt_ref[...]
    g_rows = ccol_ref[1, :, 0:1] * _softplus(smt + ccol_ref[0, :, 0:1])

    rowi = lax.broadcasted_iota(jnp.int32, (c, c), 0)
    coli = lax.broadcasted_iota(jnp.int32, (c, c), 1)
    incl = rowi >= coli
    strict = rowi > coli
    tri_l = jnp.where(incl, 1.0, 0.0).astype(BF16)
    tri_u = jnp.where(rowi <= coli, 1.0, 0.0).astype(BF16)
    ng = ng_ref[...]
    n_chunks = tr // c
    items = [(ci, hh) for ci in range(n_chunks) for hh in range(g_heads)]

    gc_cols, gc_rowsl = [], []
    for ci in range(n_chunks):
        sl = slice(ci * c, (ci + 1) * c)
        gh, gm, gl = _split3(g_all[sl, :])
        gc_cols.append(_dot(tri_l, gh) + _dot(tri_l, gm) + _dot(tri_l, gl))
        rh, rm, rl = _split3(g_rows[:, sl])
        gc_rowsl.append(_dot(rh, tri_u) + _dot(rm, tri_u) + _dot(rl, tri_u))

    qn, kn, vv, beta, gcc, gamma, qk, xm = {}, {}, {}, {}, {}, {}, {}, {}
    for it in items:
        ci, hh = it
        sl = slice(ci * c, (ci + 1) * c)
        hs = slice(hh * HEAD_DIM, (hh + 1) * HEAD_DIM)
        q = qs_ref[sl, hs]
        k = ks_ref[sl, hs]
        vv[it] = vs_ref[sl, hs]
        q = q * lax.rsqrt(jnp.sum(q * q, axis=-1, keepdims=True) + EPS) * (HEAD_DIM ** -0.5)
        k = k * lax.rsqrt(jnp.sum(k * k, axis=-1, keepdims=True) + EPS)
        qn[it], kn[it] = q, k
        beta[it] = beta_all[sl, hh:hh + 1]
        gcc[it] = gc_cols[ci][:, g_heads + hh:g_heads + hh + 1]
        gc_r = gc_rowsl[ci][g_heads + hh:g_heads + hh + 1, :]
        gamma[it] = jnp.exp(jnp.where(incl, gcc[it] - gc_r, -jnp.inf))
        kb = k.astype(BF16)
        qkk = _dot_nt(jnp.concatenate([q.astype(BF16), kb], axis=0), kb)
        qk[it] = qkk[:c]
        xm[it] = -jnp.where(strict, beta[it] * qkk[c:] * gamma[it], 0.0)

    n_lvls = int(np.log2(c)) - 3
    same8 = (rowi // 8) == (coli // 8)
    nm = {}
    for it in items:
        x8 = jnp.where(same8, xm[it], 0.0)
        xb = x8.astype(BF16)
        p1 = _dot(xb, xb)
        r = _dot(jnp.concatenate([x8, p1], axis=0).astype(BF16), p1.astype(BF16))
        n3 = x8 + p1 + r[:c]
        p2 = r[c:]
        nm[it] = n3 + p2 + _dot(n3.astype(BF16), p2.astype(BF16))
    for lvl in range(n_lvls):
        inner = (rowi // (8 << lvl)) == (coli // (8 << lvl))
        outer = (rowi // (16 << lvl)) == (coli // (16 << lvl))
        emask = outer & jnp.logical_not(inner)
        for it in items:
            e = jnp.where(emask, -xm[it], 0.0)
            m1 = e + _dot(nm[it].astype(BF16), e.astype(BF16))
            ded = m1 + _dot(m1.astype(BF16), nm[it].astype(BF16))
            nm[it] = nm[it] - ded

    q_eff, o_0, d_mat, c_mat, g_last = {}, {}, {}, {}, {}
    for it in items:
        e_gc = jnp.exp(gcc[it])
        rhs = jnp.concatenate([kn[it] * (beta[it] * e_gc), vv[it] * beta[it]], axis=1)
        solb = (rhs + _dot(nm[it].astype(BF16), rhs.astype(BF16))).astype(BF16)
        attn = (qk[it] * gamma[it]).astype(BF16)
        aw = _dot(attn, solb)
        gc_last = gcc[it][c - 1:c, :]
        k_dec = (kn[it] * jnp.exp(gc_last - gcc[it])).astype(BF16)
        kw = _dot_tn(k_dec, solb)
        q_eff[it] = (qn[it] * e_gc - aw[:, :HEAD_DIM]).astype(BF16)
        o_0[it] = aw[:, HEAD_DIM:]
        d_mat[it] = kw[:, :HEAD_DIM].astype(BF16)
        c_mat[it] = kw[:, HEAD_DIM:]
        g_last[it] = jnp.exp(gc_last)

    states = [state_ref[hh] for hh in range(g_heads)]
    for it in items:
        ci, hh = it
        sl = slice(ci * c, (ci + 1) * c)
        hs = slice(hh * HEAD_DIM, (hh + 1) * HEAD_DIM)
        st = states[hh]
        stb = st.astype(BF16)
        o = _dot(q_eff[it], stb) + o_0[it]
        states[hh] = st * g_last[it] + (c_mat[it] - _dot(d_mat[it], stb))
        o = _rms_rows(o) * ng * _silu(z_ref[sl, hs].astype(F32))
        o_ref[sl, hs] = o.astype(o_ref.dtype)
    for hh in range(g_heads):
        state_ref[hh] = states[hh]


def deltanet_mix(p, conv_w8, small, small_t, crow, ccol, norm_g, batch, seq, n_heads,
                 q_blk, k_blk, v_blk, z_blk, tr=512):
    t = p.shape[0]
    tr = min(tr, seq)
    ns = seq // tr
    ngroups = n_heads // DN_GROUP
    gw = DN_GROUP * HEAD_DIM
    kern = functools.partial(_dn_kernel, tr=tr)
    row = lambda b, g, s: b * ns + s
    return pl.pallas_call(
        kern,
        out_shape=jax.ShapeDtypeStruct((t, n_heads * HEAD_DIM), BF16),
        grid=(batch, ngroups, ns),
        in_specs=[pl.BlockSpec((tr, gw), lambda b, g, s: (row(b, g, s), q_blk + g)),
                  pl.BlockSpec((tr, gw), lambda b, g, s: (row(b, g, s), k_blk + g)),
                  pl.BlockSpec((tr, gw), lambda b, g, s: (row(b, g, s), v_blk + g)),
                  pl.BlockSpec((tr, gw), lambda b, g, s: (row(b, g, s), z_blk + g)),
                  pl.BlockSpec((8, gw), lambda b, g, s: (0, q_blk + g)),
                  pl.BlockSpec((8, gw), lambda b, g, s: (0, k_blk + g)),
                  pl.BlockSpec((8, gw), lambda b, g, s: (0, v_blk + g)),
                  pl.BlockSpec((tr, LANES), lambda b, g, s: (row(b, g, s), g)),
                  pl.BlockSpec((None, None, 8, tr), lambda b, g, s: (b, g, 0, s)),
                  pl.BlockSpec((None, 8, LANES), lambda b, g, s: (g, 0, 0)),
                  pl.BlockSpec((None, 2, 8, LANES), lambda b, g, s: (g, 0, 0, 0)),
                  pl.BlockSpec((1, HEAD_DIM), lambda b, g, s: (0, 0))],
        out_specs=pl.BlockSpec((tr, gw), lambda b, g, s: (row(b, g, s), g)),
        scratch_shapes=[pltpu.VMEM((DN_GROUP, HEAD_DIM, HEAD_DIM), F32),
                        pltpu.VMEM((tr + 8, gw), F32), pltpu.VMEM((tr + 8, gw), F32),
                        pltpu.VMEM((tr + 8, gw), F32),
                        pltpu.VMEM((tr, gw), F32), pltpu.VMEM((tr, gw), F32), pltpu.VMEM((tr, gw), F32)],
        compiler_params=_cparams(3), name="deltanet",
    )(p, p, p, p, conv_w8, conv_w8, conv_w8, small, small_t, crow, ccol, norm_g)


def _norm_router_kernel(x_ref, g_ref, sc_ref, sh_ref, wr_ref, br_ref, h_ref, lg_ref):
    x = x_ref[...]
    h = _rms_rows(x) * g_ref[...] * (1.0 + sc_ref[0]) + sh_ref[0]
    half = h.shape[1] // 2
    h_ref[...] = _bf16_bits_lo(h[:, :half]) | _bf16_bits_hi(h[:, half:])
    w = wr_ref[...]
    w_hi = w.astype(BF16)
    w_lo = (w - w_hi.astype(F32)).astype(BF16)
    h_hi = h.astype(BF16)
    h_lo = (h - h_hi.astype(F32)).astype(BF16)
    lg_ref[...] = _dot(h_hi, w_hi) + _dot(h_lo, w_hi) + _dot(h_hi, w_lo) + br_ref[...]


def norm_router(x2d, g, sc, sh, w_router_pad, b_router_pad, seq, tm=256):
    t, d = x2d.shape
    tm = min(tm, seq)
    nb = seq // tm
    return pl.pallas_call(
        _norm_router_kernel,
        out_shape=(jax.ShapeDtypeStruct((t, d // 2), U32), jax.ShapeDtypeStruct((t, LANES), F32)),
        grid=(t // tm,),
        in_specs=[pl.BlockSpec((tm, d), lambda i: (i, 0)),
                  pl.BlockSpec((1, d), lambda i: (0, 0)),
                  pl.BlockSpec((1, 1, d), lambda i: (i // nb, 0, 0)),
                  pl.BlockSpec((1, 1, d), lambda i: (i // nb, 0, 0)),
                  pl.BlockSpec((d, LANES), lambda i: (0, 0)),
                  pl.BlockSpec((1, LANES), lambda i: (0, 0))],
        out_specs=(pl.BlockSpec((tm, d // 2), lambda i: (i, 0)),
                   pl.BlockSpec((tm, LANES), lambda i: (i, 0))),
        compiler_params=_cparams(1), name="norm_router",
    )(x2d, g.reshape(1, d), sc, sh, w_router_pad, b_router_pad)


def _moe_kernel(blk_e_ref, first_ref, nxt_ref, nused_ref, tok_ref, tokn_ref, h_hbm, wgu_hbm, wd_hbm,
                bgu_ref, bd_ref, y_ref,
                stg_gu, stg_d, wgu_b, wd_b, perm_ref, xbuf0, xbuf1, wsem, gsem, *, tb, layer):
    i = pl.program_id(0)
    nused = nused_ref[0]
    d, f2 = stg_gu.shape
    f = f2 // 2
    xbufs = (xbuf0, xbuf1)

    def weight_copies(e):
        return (pltpu.make_async_copy(wgu_hbm.at[layer, e], stg_gu, wsem.at[0]),
                pltpu.make_async_copy(wd_hbm.at[layer, e], stg_d, wsem.at[1]))

    def row_copy(toks, r, s):
        return pltpu.make_async_copy(h_hbm.at[pl.ds(toks[0, r], 1), :],
                                     xbufs[s].at[pl.ds(r, 1), :], gsem.at[s])

    def wait_gather(s):
        pltpu.make_async_copy(h_hbm.at[pl.ds(0, tb), :], xbufs[s], gsem.at[s]).wait()

    @pl.when(i == 0)
    def _():
        for cp in weight_copies(blk_e_ref[0]):
            cp.start()

        def issue(r, carry):
            row_copy(tok_ref, r, 0).start()
            return carry
        lax.fori_loop(0, tb, issue, 0)
        rr = lax.broadcasted_iota(jnp.int32, (f2, f), 0)
        cc = lax.broadcasted_iota(jnp.int32, (f2, f), 1)
        perm_ref[...] = jnp.where(rr == 2 * cc, 1.0, 0.0).astype(BF16)

    @pl.when((i < nused) & (first_ref[i] == 1))
    def _():
        for cp in weight_copies(0):
            cp.wait()
        rows = min(256, f)

        def cast_gu(r, carry):
            r0 = pl.multiple_of(r * rows, rows)
            wgu_b[pl.ds(r0, rows), :] = stg_gu[pl.ds(r0, rows), :].astype(BF16)
            return carry
        lax.fori_loop(0, d // rows, cast_gu, 0)

        def cast_d(r, carry):
            r0 = pl.multiple_of(r * rows, rows)
            wd_b[pl.ds(r0, rows), :] = stg_d[pl.ds(r0, rows), :].astype(BF16)
            return carry
        lax.fori_loop(0, f // rows, cast_d, 0)

        @pl.when(nxt_ref[i] >= 0)
        def _():
            for cp in weight_copies(nxt_ref[i]):
                cp.start()

    def compute(s):
        wait_gather(s)
        xw = xbufs[s][...]
        x = jnp.concatenate([pltpu.bitcast(xw << 16, F32).astype(BF16),
                             pltpu.bitcast(xw & jnp.uint32(HI16), F32).astype(BF16)], axis=1)
        for r in range(tb):
            row_copy(tokn_ref, r, 1 - s).start()
        gu = _dot(x, wgu_b[...]) + bgu_ref[...]
        gate = jnp.minimum(gu, SWIGLU_LIMIT)
        sg = gate * _sigmoid(SWIGLU_ALPHA * gate)
        lin1 = jnp.clip(gu, -SWIGLU_LIMIT, SWIGLU_LIMIT) + 1.0
        act_il = sg * pltpu.roll(lin1, f2 - 1, 1)
        act = _dot(act_il.astype(BF16), perm_ref[...])
        y = _dot(act.astype(BF16), wd_b[...]) + bd_ref[...]
        y_ref[...] = _bf16_bits_lo(y[:, :d // 2]) | _bf16_bits_hi(y[:, d // 2:])

    for s in range(2):
        @pl.when((i < nused) & (i % 2 == s))
        def _(s=s):
            compute(s)

        @pl.when((i == nused) & (i % 2 == s))
        def _(s=s):
            wait_gather(s)

    @pl.when(i >= nused)
    def _():
        y_ref[...] = jnp.zeros_like(y_ref)


def moe_experts(h2, w_gu, w_down, b_gu_l, b_down_l, layer, blk_e, first, nxt, n_used, row_tok, tb):
    d = w_gu.shape[2]
    n_blk = row_tok.shape[0]
    f2 = w_gu.shape[3]
    f = f2 // 2
    kern = functools.partial(_moe_kernel, tb=tb, layer=layer)
    grid_spec = pltpu.PrefetchScalarGridSpec(
        num_scalar_prefetch=4,
        grid=(n_blk,),
        in_specs=[pl.BlockSpec((None, 1, tb), lambda i, *_: (i, 0, 0), memory_space=pltpu.SMEM),
                  pl.BlockSpec((None, 1, tb), lambda i, *_: (jnp.minimum(i + 1, n_blk - 1), 0, 0),
                               memory_space=pltpu.SMEM),
                  pl.BlockSpec(memory_space=pl.ANY),
                  pl.BlockSpec(memory_space=pl.ANY),
                  pl.BlockSpec(memory_space=pl.ANY),
                  pl.BlockSpec((None, 1, f2), lambda i, be, *_: (be[i], 0, 0)),
                  pl.BlockSpec((None, 1, d), lambda i, be, *_: (be[i], 0, 0))],
        out_specs=pl.BlockSpec((tb, d // 2), lambda i, *_: (i, 0)),
        scratch_shapes=[pltpu.VMEM((d, f2), F32), pltpu.VMEM((f, d), F32),
                        pltpu.VMEM((d, f2), BF16), pltpu.VMEM((f, d), BF16),
                        pltpu.VMEM((f2, f), BF16), pltpu.VMEM((tb, d // 2), U32),
                        pltpu.VMEM((tb, d // 2), U32),
                        pltpu.SemaphoreType.DMA((2,)), pltpu.SemaphoreType.DMA((2,))])
    return pl.pallas_call(
        kern,
        out_shape=jax.ShapeDtypeStruct((n_blk * tb, d // 2), U32),
        grid_spec=grid_spec,
        compiler_params=_cparams(1), name="moe_experts",
    )(blk_e, first, nxt, n_used, row_tok, row_tok, h2, w_gu, w_down, b_gu_l, b_down_l)


def _combine_kernel(pos_ref, posn_ref, y_hbm, x_ref, tw_ref, g2_ref, ng_ref, sc_ref, sh_ref, *rest,
                    tm, final, n_steps):
    if final:
        out_ref, ybuf0, ybuf1, sem = rest
    else:
        xo_ref, h_ref, ybuf0, ybuf1, sem = rest
    i = pl.program_id(0)
    ybufs = (ybuf0, ybuf1)

    def start_gather(p_ref, s):
        def issue(r, carry):
            for k in range(TOP_K):
                pltpu.make_async_copy(y_hbm.at[pl.ds(p_ref[0, r * TOP_K + k], 1), :],
                                      ybufs[s].at[k, pl.ds(r, 1), :], sem.at[s]).start()
            return carry
        lax.fori_loop(0, tm, issue, 0)

    def wait_gather(s):
        for k in range(TOP_K):
            pltpu.make_async_copy(y_hbm.at[pl.ds(0, tm), :], ybufs[s].at[k], sem.at[s]).wait()

    def lo(w):
        return pltpu.bitcast(w << 16, F32)

    def hi(w):
        return pltpu.bitcast(w & jnp.uint32(HI16), F32)

    def finish(s):
        tw = tw_ref[...]
        acc_lo = acc_hi = None
        for k in range(TOP_K):
            w = ybufs[s][k]
            wk = tw[:, k:k + 1]
            acc_lo = wk * lo(w) if acc_lo is None else acc_lo + wk * lo(w)
            acc_hi = wk * hi(w) if acc_hi is None else acc_hi + wk * hi(w)
        x = x_ref[...] + g2_ref[0] * jnp.concatenate([acc_lo, acc_hi], axis=1)
        if final:
            out_ref[...] = _rms_rows(x) * ng_ref[...]
        else:
            xo_ref[...] = x
            h = _rms_rows(x) * ng_ref[...] * (1.0 + sc_ref[0]) + sh_ref[0]
            h_ref[...] = h.astype(h_ref.dtype)

    @pl.when(i == 0)
    def _():
        start_gather(pos_ref, 0)

    for s in range(2):
        @pl.when(i % 2 == s)
        def _(s=s):
            @pl.when(i + 1 < n_steps)
            def _():
                start_gather(posn_ref, 1 - s)
            wait_gather(s)
            finish(s)


def moe_combine(y, pos, top_w_pad, x2d, gate2, norm_g, sc, sh, seq, final, tm=128):
    t, d = x2d.shape
    tm = min(tm, seq)
    nb = seq // tm
    n_steps = t // tm
    kern = functools.partial(_combine_kernel, tm=tm, final=final, n_steps=n_steps)
    if final:
        out_shape = jax.ShapeDtypeStruct((t, d), F32)
        out_specs = pl.BlockSpec((tm, d), lambda i: (i, 0))
    else:
        out_shape = (jax.ShapeDtypeStruct((t, d), F32), jax.ShapeDtypeStruct((t, d), BF16))
        out_specs = (pl.BlockSpec((tm, d), lambda i: (i, 0)), pl.BlockSpec((tm, d), lambda i: (i, 0)))
    pos3 = pos.reshape(n_steps, 1, tm * TOP_K)
    return pl.pallas_call(
        kern,
        out_shape=out_shape,
        grid=(n_steps,),
        in_specs=[pl.BlockSpec((None, 1, tm * TOP_K), lambda i: (i, 0, 0), memory_space=pltpu.SMEM),
                  pl.BlockSpec((None, 1, tm * TOP_K), lambda i: (jnp.minimum(i + 1, n_steps - 1), 0, 0),
                               memory_space=pltpu.SMEM),
                  pl.BlockSpec(memory_space=pl.ANY),
                  pl.BlockSpec((tm, d), lambda i: (i, 0)),
                  pl.BlockSpec((tm, LANES), lambda i: (i, 0)),
                  pl.BlockSpec((1, 1, d), lambda i: (i // nb, 0, 0)),
                  pl.BlockSpec((1, d), lambda i: (0, 0)),
                  pl.BlockSpec((1, 1, d), lambda i: (i // nb, 0, 0)),
                  pl.BlockSpec((1, 1, d), lambda i: (i // nb, 0, 0))],
        out_specs=out_specs,
        scratch_shapes=[pltpu.VMEM((TOP_K, tm, d // 2), U32), pltpu.VMEM((TOP_K, tm, d // 2), U32),
                        pltpu.SemaphoreType.DMA((2,))],
        compiler_params=_cparams(1), name="moe_combine",
    )(pos3, pos3, y, x2d, top_w_pad, gate2, norm_g.reshape(1, d), sc, sh)


def _routing_tables(logits, n_experts, tb):
    t = logits.shape[0]
    top_v, top_e = lax.top_k(logits[:, :n_experts], TOP_K)
    top_w = jax.nn.softmax(top_v, axis=-1)
    n_a = t * TOP_K
    eids = jnp.arange(n_experts, dtype=jnp.int32)
    sel = jnp.any(top_e[:, :, None] == eids[None, None, :], axis=1).astype(jnp.int32)
    csum = jnp.cumsum(sel, axis=0)
    counts = csum[-1]
    rank = jnp.take_along_axis(csum - sel, top_e, axis=1)
    padded = (counts + tb - 1) // tb * tb
    pend = jnp.cumsum(padded)
    pstart = pend - padded
    pos = (pstart[top_e] + rank).astype(jnp.int32).reshape(n_a)
    n_rows = (n_a + n_experts * (tb - 1) + tb - 1) // tb * tb + tb
    n_blk = n_rows // tb
    t_a = jnp.repeat(jnp.arange(t, dtype=jnp.int32), TOP_K)
    row_tok = jnp.zeros((n_rows,), jnp.int32).at[pos].set(t_a)
    blk_start = jnp.arange(n_blk, dtype=jnp.int32) * tb
    blk_e = jnp.minimum(jnp.sum((pend[None, :] <= blk_start[:, None]).astype(jnp.int32), axis=1),
                        n_experts - 1).astype(jnp.int32)
    n_used = (pend[-1] // tb).astype(jnp.int32)
    used = jnp.arange(n_blk) < n_used
    first = (used & (blk_start == pstart[blk_e])).astype(jnp.int32)
    has = counts > 0
    cand = jnp.where(has, eids, n_experts)
    suffix_min = lax.cummin(cand, axis=0, reverse=True)
    nxt_e = jnp.concatenate([suffix_min[1:], jnp.full((1,), n_experts, jnp.int32)])
    nxt_e = jnp.where(nxt_e >= n_experts, -1, nxt_e).astype(jnp.int32)
    nxt = nxt_e[blk_e]
    top_w_pad = jnp.pad(top_w, ((0, 0), (0, LANES - TOP_K)))
    return pos, top_w_pad, row_tok.reshape(n_blk, 1, tb), blk_e, first, nxt, n_used.reshape(1)


def _rope_tables(seq):
    half = HEAD_DIM // 2
    inv = ROPE_BASE ** (-jnp.arange(half, dtype=F32) / half)
    ang = jnp.arange(seq, dtype=F32)[:, None] * inv[None, :]
    cos, sin = jnp.cos(ang), jnp.sin(ang)
    return jnp.concatenate([cos, cos], axis=-1), jnp.concatenate([-sin, sin], axis=-1)


def kernel(x, c, norm1_g, w_ada, b_ada, w_in, dn_conv_w, dn_a_log, dn_dt_bias, dn_norm_g, fox_f_bias,
           w_out, norm2_g, w_router, b_router, w_gu, b_gu, w_down, b_down, final_norm_g):
    batch, seq, d = x.shape
    depth = w_ada.shape[0]
    t = batch * seq
    dn_heads = dn_a_log.shape[1]
    fox_heads = fox_f_bias.shape[1]
    ret_heads = d // HEAD_DIM - dn_heads - fox_heads
    dn_w, ret_w, fox_w = dn_heads * HEAD_DIM, ret_heads * HEAD_DIM, fox_heads * HEAD_DIM
    n_experts = w_router.shape[2]
    n_groups = dn_heads // DN_GROUP
    gw = DN_GROUP * HEAD_DIM
    moe_tb = 256
    tn = 512

    widths = (3 * dn_w, dn_w, dn_heads, dn_heads, ret_w, ret_w, ret_w, ret_w,
              fox_w, fox_w, fox_w, fox_w, fox_heads)
    cuts = np.concatenate([[0], np.cumsum(widths)])
    o_b, o_a, o_r, o_ff = int(cuts[2]), int(cuts[3]), int(cuts[4]), int(cuts[12])
    n_main = o_b + (o_ff - o_r)
    small_src, small_dst = [], []
    for g in range(n_groups):
        for hh in range(DN_GROUP):
            small_src += [g * DN_GROUP + hh, dn_heads + g * DN_GROUP + hh]
            small_dst += [g * LANES + hh, g * LANES + DN_GROUP + hh]
    for hh in range(fox_heads):
        small_src.append(2 * dn_heads + hh)
        small_dst.append(n_groups * LANES + hh)
    small_src = np.asarray(small_src, np.int32)
    small_dst = np.asarray(small_dst, np.int32)
    w_in_t = jnp.transpose(w_in, (2, 0, 1))
    w_main_t = w_in_prep(w_in_t, o_b, o_r - o_b, n_main)
    w_sc = jnp.concatenate([w_in_t[o_b:o_r], w_in_t[o_ff:]], axis=0)
    w_small_t = jnp.zeros((depth, (n_groups + 1) * LANES, d), BF16).at[:, small_dst].set(
        jnp.transpose(w_sc, (1, 0, 2))[:, small_src].astype(BF16))

    x2d = x.reshape(t, d)
    c_pad = jnp.zeros((8, d), BF16).at[:batch].set(c.astype(BF16))
    mod = ada_mod(c_pad, w_ada, b_ada)[:, :batch, :]
    mod = mod.reshape(depth, batch, 6, 1, d)
    cos2, sin2 = _rope_tables(seq)
    lg_tab = jnp.log(1.0 - 2.0 ** (-5.0 - jnp.arange(ret_heads, dtype=F32)))
    lg_tab = jnp.broadcast_to(lg_tab[:, None, None], (ret_heads, 8, LANES))

    dn_q_blk, dn_k_blk, dn_v_blk, dn_z_blk = 0, dn_w // gw, 2 * dn_w // gw, 3 * dn_w // gw
    r0 = o_b // HEAD_DIM
    r_q_blk, r_k_blk, r_v_blk, r_g_blk = r0, r0 + ret_heads, r0 + 2 * ret_heads, r0 + 3 * ret_heads
    f0 = r0 + 4 * ret_heads
    f_q_blk, f_k_blk, f_v_blk, f_z_blk = f0, f0 + fox_heads, f0 + 2 * fox_heads, f0 + 3 * fox_heads

    h = norm_mod(x2d, norm1_g[0], mod[0, :, 1], mod[0, :, 0], seq)
    out = None
    for l in range(depth):
        sh1, sc1, g1, sh2, sc2, g2 = (mod[l, :, i] for i in range(6))
        p = matmul_nt(h, w_main_t, l, BF16, "in_proj")
        small = matmul_nt(h, w_small_t, l, F32, "small_proj")

        conv_w8 = jnp.zeros((8, 3 * dn_w), F32).at[:CONV_K].set(dn_conv_w[l])
        small_t = small[:, :n_groups * LANES].reshape(batch, seq, n_groups, LANES)[..., :8]
        small_t = small_t.transpose(0, 2, 3, 1)
        dtb = dn_dt_bias[l].reshape(n_groups, DN_GROUP)
        nega = -jnp.exp(dn_a_log[l]).reshape(n_groups, DN_GROUP)
        zg = jnp.zeros((n_groups, DN_GROUP), F32)
        pad = jnp.zeros((n_groups, LANES - 2 * DN_GROUP), F32)
        crow = jnp.stack([jnp.concatenate([zg, dtb, pad], axis=1),
                          jnp.concatenate([zg, nega, pad], axis=1)], axis=1)
        crow = jnp.concatenate([crow, jnp.zeros((n_groups, 6, LANES), F32)], axis=1)
        ccol = jnp.stack([jnp.concatenate([zg, dtb], axis=1), jnp.concatenate([zg, nega], axis=1)], axis=1)
        ccol = jnp.broadcast_to(ccol[..., None], (n_groups, 2, 8, LANES))
        o_dn = deltanet_mix(p, conv_w8, small, small_t, crow, ccol, dn_norm_g[l].reshape(1, HEAD_DIM),
                            batch, seq, dn_heads, dn_q_blk, dn_k_blk, dn_v_blk, dn_z_blk)
        o_ret = retention_mix(p, cos2, sin2, lg_tab, batch, seq, ret_heads,
                              r_q_blk, r_k_blk, r_v_blk, r_g_blk)
        fb = jnp.zeros((1, LANES), F32).at[0, :fox_heads].set(fox_f_bias[l])
        f_cum = fox_prefix(small, fb, batch, seq, n_groups)
        o_fox = fox_attention(p, f_cum, batch, seq, fox_heads, f_q_blk, f_k_blk, f_v_blk, f_z_blk)
        x2d = out_proj_resid(o_dn, o_ret, o_fox, w_out, l, x2d, g1, seq)

        wr_pad = jnp.zeros((d, LANES), F32).at[:, :n_experts].set(w_router[l])
        br_pad = jnp.zeros((1, LANES), F32).at[0, :n_experts].set(b_router[l])
        h2, logits = norm_router(x2d, norm2_g[l], sc2, sh2, wr_pad, br_pad, seq)
        pos, top_w_pad, row_tok, blk_e, first, nxt, n_used = _routing_tables(logits, n_experts, moe_tb)
        y = moe_experts(h2, w_gu, w_down, b_gu[l][:, None, :], b_down[l][:, None, :], l,
                        blk_e, first, nxt, n_used, row_tok, moe_tb)
        if l + 1 < depth:
            nsh1, nsc1 = mod[l + 1, :, 0], mod[l + 1, :, 1]
            x2d, h = moe_combine(y, pos, top_w_pad, x2d, g2, norm1_g[l + 1], nsc1, nsh1, seq, final=False)
        else:
            out = moe_combine(y, pos, top_w_pad, x2d, g2, final_norm_g, sc2, sh2, seq, final=True)
    return out.reshape(batch, seq, d)
```

```python
import functools

import numpy as np
import jax
import jax.numpy as jnp
from jax import lax
from jax.experimental import pallas as pl
from jax.experimental.pallas import tpu as pltpu

F32 = jnp.float32
BF16 = jnp.bfloat16
U32 = jnp.uint32

HEAD_DIM = 128
CONV_K = 4
DN_CHUNK = 128
RET_CHUNK = 128
ROPE_BASE = 10000.0
TOP_K = 4
SWIGLU_LIMIT = 7.0
SWIGLU_ALPHA = 1.702
EPS = 1e-6
DN_GROUP = 4
LANES = 128
VMEM_LIMIT = 56 * 1024 * 1024
MOE_VMEM_LIMIT = 60 * 1024 * 1024
LOG2E = 1.4426950408889634
HI16 = 0xFFFF0000


def _cparams(n_axes, vmem_limit=VMEM_LIMIT):
    return pltpu.CompilerParams(dimension_semantics=("arbitrary",) * n_axes,
                                vmem_limit_bytes=vmem_limit)


def _dot(a, b):
    return jnp.dot(a, b, preferred_element_type=F32)


def _dot_nt(a, b):
    return lax.dot_general(a, b, (((1,), (1,)), ((), ())), preferred_element_type=F32)


def _dot_tn(a, b):
    return lax.dot_general(a, b, (((0,), (0,)), ((), ())), preferred_element_type=F32)


def _split3(x):
    h = x.astype(BF16)
    r = x - h.astype(F32)
    m = r.astype(BF16)
    l = (r - m.astype(F32)).astype(BF16)
    return h, m, l


def _sigmoid(x):
    return 1.0 / (1.0 + jnp.exp(-x))


def _silu(x):
    return x * _sigmoid(x)


def _softplus(x):
    return jnp.maximum(x, 0.0) + jnp.log1p(jnp.exp(-jnp.abs(x)))


def _log_sigmoid(x):
    return jnp.minimum(x, 0.0) - jnp.log1p(jnp.exp(-jnp.abs(x)))


def _rms_rows(x):
    return x * lax.rsqrt(jnp.mean(x * x, axis=-1, keepdims=True) + EPS)


def _bf16_bits_lo(x):
    return pltpu.bitcast(x.astype(BF16).astype(F32), U32) >> 16


def _bf16_bits_hi(x):
    return pltpu.bitcast(x.astype(BF16).astype(F32), U32) & jnp.uint32(HI16)


def _ada_kernel(c_ref, w_ref, b_ref, o_ref):
    o_ref[...] = _dot(c_ref[...], w_ref[...].astype(BF16)) + b_ref[...]


def ada_mod(c_pad, w_ada, b_ada, tn=512):
    depth, d, n = w_ada.shape
    return pl.pallas_call(
        _ada_kernel,
        out_shape=jax.ShapeDtypeStruct((depth, 8, n), F32),
        grid=(depth, n // tn),
        in_specs=[pl.BlockSpec((8, d), lambda l, j: (0, 0)),
                  pl.BlockSpec((None, d, tn), lambda l, j: (l, 0, j)),
                  pl.BlockSpec((None, 1, tn), lambda l, j: (l, 0, j))],
        out_specs=pl.BlockSpec((None, 8, tn), lambda l, j: (l, 0, j)),
        compiler_params=_cparams(2), name="ada_mod",
    )(c_pad, w_ada, b_ada.reshape(depth, 1, n))


def _norm_mod_kernel(x_ref, g_ref, sc_ref, sh_ref, h_ref):
    x = x_ref[...]
    h = _rms_rows(x) * g_ref[...] * (1.0 + sc_ref[0]) + sh_ref[0]
    h_ref[...] = h.astype(h_ref.dtype)


def norm_mod(x2d, g, sc, sh, seq, tm=256):
    t, d = x2d.shape
    tm = min(tm, seq)
    nb = seq // tm
    return pl.pallas_call(
        _norm_mod_kernel,
        out_shape=jax.ShapeDtypeStruct((t, d), BF16),
        grid=(t // tm,),
        in_specs=[pl.BlockSpec((tm, d), lambda i: (i, 0)),
                  pl.BlockSpec((1, d), lambda i: (0, 0)),
                  pl.BlockSpec((1, 1, d), lambda i: (i // nb, 0, 0)),
                  pl.BlockSpec((1, 1, d), lambda i: (i // nb, 0, 0))],
        out_specs=pl.BlockSpec((tm, d), lambda i: (i, 0)),
        compiler_params=_cparams(1), name="norm_mod",
    )(x2d, g.reshape(1, d), sc, sh)


def _w_prep_kernel(w_ref, o_ref):
    for l in range(o_ref.shape[0]):
        o_ref[l] = w_ref[:, l, :].astype(BF16)


def w_in_prep(w_in_t, gap_start, gap_len, n_rows, rows=256):
    n, depth, k = w_in_t.shape
    rows = min(rows, gap_start)
    assert gap_start % rows == 0 and n_rows % rows == 0 and gap_len % 8 == 0

    def in_idx(j):
        r = j * rows
        return (r + jnp.where(r >= gap_start, gap_len, 0), 0, 0)
    return pl.pallas_call(
        _w_prep_kernel,
        out_shape=jax.ShapeDtypeStruct((depth, n_rows, k), BF16),
        grid=(n_rows // rows,),
        in_specs=[pl.BlockSpec((pl.Element(rows), pl.Element(depth), pl.Element(k)), in_idx)],
        out_specs=pl.BlockSpec((depth, rows, k), lambda j: (0, j, 0)),
        compiler_params=_cparams(1), name="w_in_prep",
    )(w_in_t)


def _mm_nt_kernel(x_ref, w_ref, o_ref):
    o_ref[...] = _dot_nt(x_ref[...], w_ref[...]).astype(o_ref.dtype)


def matmul_nt(x, w_t, layer, out_dtype, name, tm=1024, tn=512):
    m, k = x.shape
    n = w_t.shape[1]
    tm = min(tm, m)
    tn = min(tn, n)
    return pl.pallas_call(
        _mm_nt_kernel,
        out_shape=jax.ShapeDtypeStruct((m, n), out_dtype),
        grid=(n // tn, m // tm),
        in_specs=[pl.BlockSpec((tm, k), lambda j, i: (i, 0)),
                  pl.BlockSpec((None, tn, k), lambda j, i: (layer, j, 0))],
        out_specs=pl.BlockSpec((tm, tn), lambda j, i: (i, j)),
        compiler_params=_cparams(2), name=name,
    )(x, w_t)


def _mm_resid_kernel(a_ref, b_ref, c_ref, w_ref, x_ref, g_ref, o_ref, wb_ref, *, splits):
    @pl.when(pl.program_id(1) == 0)
    def _():
        wb_ref[...] = w_ref[...].astype(BF16)
    k0, k1 = splits
    y = (_dot(a_ref[...], wb_ref[0:k0, :]) + _dot(b_ref[...], wb_ref[k0:k1, :])
         + _dot(c_ref[...], wb_ref[k1:, :]))
    o_ref[...] = x_ref[...] + g_ref[0] * y


def out_proj_resid(a, b, c, w_out, layer, x2d, gate, seq, tm=1024, tn=512):
    m = a.shape[0]
    ka, kb, kc = a.shape[1], b.shape[1], c.shape[1]
    k = ka + kb + kc
    n = w_out.shape[2]
    tm = min(tm, seq)
    nb = seq // tm
    kern = functools.partial(_mm_resid_kernel, splits=(ka, ka + kb))
    return pl.pallas_call(
        kern,
        out_shape=jax.ShapeDtypeStruct((m, n), F32),
        grid=(n // tn, m // tm),
        in_specs=[pl.BlockSpec((tm, ka), lambda j, i: (i, 0)),
                  pl.BlockSpec((tm, kb), lambda j, i: (i, 0)),
                  pl.BlockSpec((tm, kc), lambda j, i: (i, 0)),
                  pl.BlockSpec((None, k, tn), lambda j, i: (layer, 0, j)),
                  pl.BlockSpec((tm, tn), lambda j, i: (i, j)),
                  pl.BlockSpec((1, 1, tn), lambda j, i: (i // nb, 0, j))],
        out_specs=pl.BlockSpec((tm, tn), lambda j, i: (i, j)),
        scratch_shapes=[pltpu.VMEM((k, tn), BF16)],
        compiler_params=_cparams(2), name="out_proj",
    )(a, b, c, w_out, x2d, gate)


def _fprep_kernel(s_ref, b_ref, f_ref, carry_ref):
    @pl.when(pl.program_id(1) == 0)
    def _():
        carry_ref[...] = jnp.zeros_like(carry_ref)
    tq = s_ref.shape[0]
    lf = _log_sigmoid(s_ref[...] + b_ref[...])
    row = lax.broadcasted_iota(jnp.int32, (tq, tq), 0)
    col = lax.broadcasted_iota(jnp.int32, (tq, tq), 1)
    tri = jnp.where(row >= col, 1.0, 0.0).astype(BF16)
    h, m, l = _split3(lf)
    f = _dot(tri, h) + _dot(tri, m) + _dot(tri, l) + carry_ref[0:1, :]
    f_ref[...] = f
    carry_ref[...] = jnp.broadcast_to(f[tq - 1:tq, :], carry_ref.shape)


def fox_prefix(small, bias_row, batch, seq, col_block, tq=512):
    t = small.shape[0]
    tq = min(tq, seq)
    nq = seq // tq
    return pl.pallas_call(
        _fprep_kernel,
        out_shape=jax.ShapeDtypeStruct((t, LANES), F32),
        grid=(batch, nq),
        in_specs=[pl.BlockSpec((tq, LANES), lambda b, i: (b * nq + i, col_block)),
                  pl.BlockSpec((1, LANES), lambda b, i: (0, 0))],
        out_specs=pl.BlockSpec((tq, LANES), lambda b, i: (b * nq + i, 0)),
        scratch_shapes=[pltpu.VMEM((8, LANES), F32)],
        compiler_params=_cparams(2), name="fox_prefix",
    )(small, bias_row)


FOX_VT_ROWS = HEAD_DIM + 16


def _fox_kernel(q_ref, k_ref, v_ref, z_ref, f_ref, o_ref, ka_ref, vt_ref, m_ref, acc_ref, *, tq, seq):
    h = pl.program_id(1)
    qi = pl.program_id(2)
    lane = lax.broadcasted_iota(jnp.int32, (tq, LANES), 1)

    def f_column(rows):
        fb = f_ref[pl.ds(rows, tq), :]
        return jnp.sum(jnp.where(lane == h, fb, 0.0), axis=-1, keepdims=True) * LOG2E

    def aug_lanes(fcol, sign_first):
        fh, fm, fl = _split3(fcol)
        fh, fm, fl = fh.astype(F32), fm.astype(F32), fl.astype(F32)
        one = jnp.ones((tq, LANES), F32)
        zero = jnp.zeros((tq, LANES), F32)
        if sign_first:
            a = jnp.where(lane == 0, fh, jnp.where(lane == 1, fm, jnp.where(lane == 2, fl,
                          jnp.where(lane < 6, one, zero))))
        else:
            a = jnp.where(lane < 3, one, jnp.where(lane == 3, -fh, jnp.where(lane == 4, -fm,
                          jnp.where(lane == 5, -fl, zero))))
        return a.astype(BF16)

    @pl.when(qi == 0)
    def _():
        def build(j, carry):
            r0 = pl.multiple_of(j * tq, tq)
            ka_ref[pl.ds(r0, tq), 0:HEAD_DIM] = k_ref[pl.ds(r0, tq), :]
            ka_ref[pl.ds(r0, tq), HEAD_DIM:] = aug_lanes(f_column(r0), False)
            vt_ref[0:HEAD_DIM, pl.ds(r0, tq)] = v_ref[pl.ds(r0, tq), :].astype(F32).T.astype(BF16)
            vt_ref[HEAD_DIM:, pl.ds(r0, tq)] = jnp.ones((FOX_VT_ROWS - HEAD_DIM, tq), BF16)
            return carry
        lax.fori_loop(0, seq // tq, build, 0)

    q0 = pl.multiple_of(qi * tq, tq)
    qs = (q_ref[...].astype(F32) * (HEAD_DIM ** -0.5 * LOG2E)).astype(BF16)
    qa = jnp.concatenate([qs, aug_lanes(f_column(q0), True)], axis=1)

    m_ref[...] = jnp.full_like(m_ref, -1e30)
    acc_ref[...] = jnp.zeros_like(acc_ref)

    def block(k0, width, masked):
        st = _dot_nt(ka_ref[pl.ds(k0, width), :], qa)
        if masked:
            key = lax.broadcasted_iota(jnp.int32, (width, tq), 0)
            qry = lax.broadcasted_iota(jnp.int32, (width, tq), 1)
            st = jnp.where(qry >= key, st, -1e30)
        m_old = m_ref[...]
        m_new = jnp.maximum(m_old, jnp.max(st, axis=0, keepdims=True))
        p = jnp.exp2(st - m_new).astype(BF16)
        acc_ref[...] = (jnp.exp2(m_old - m_new) * acc_ref[...]
                        + _dot(vt_ref[:, pl.ds(k0, width)], p))
        m_ref[...] = m_new

    def wide(j, carry):
        block(pl.multiple_of(j * (2 * tq), 2 * tq), 2 * tq, False)
        return carry
    lax.fori_loop(0, qi // 2, wide, 0)

    @pl.when(qi % 2 == 1)
    def _():
        block(pl.multiple_of((qi - 1) * tq, tq), tq, False)

    block(q0, tq, True)
    acc = acc_ref[...]
    o = (acc[:HEAD_DIM] / acc[HEAD_DIM:HEAD_DIM + 1]).T
    o_ref[...] = (o * _sigmoid(z_ref[...].astype(F32))).astype(o_ref.dtype)


def fox_attention(p, f_cum, batch, seq, n_heads, q_blk, k_blk, v_blk, z_blk, tq=512):
    t = p.shape[0]
    tq = min(tq, seq)
    nq = seq // tq
    kern = functools.partial(_fox_kernel, tq=tq, seq=seq)
    return pl.pallas_call(
        kern,
        out_shape=jax.ShapeDtypeStruct((t, n_heads * HEAD_DIM), BF16),
        grid=(batch, n_heads, nq),
        in_specs=[pl.BlockSpec((tq, HEAD_DIM), lambda b, h, i: (b * nq + i, q_blk + h)),
                  pl.BlockSpec((seq, HEAD_DIM), lambda b, h, i: (b, k_blk + h)),
                  pl.BlockSpec((seq, HEAD_DIM), lambda b, h, i: (b, v_blk + h)),
                  pl.BlockSpec((tq, HEAD_DIM), lambda b, h, i: (b * nq + i, z_blk + h)),
                  pl.BlockSpec((seq, LANES), lambda b, h, i: (b, 0))],
        out_specs=pl.BlockSpec((tq, HEAD_DIM), lambda b, h, i: (b * nq + i, h)),
        scratch_shapes=[pltpu.VMEM((seq, 2 * HEAD_DIM), BF16), pltpu.VMEM((FOX_VT_ROWS, seq), BF16),
                        pltpu.VMEM((1, tq), F32), pltpu.VMEM((FOX_VT_ROWS, tq), F32)],
        compiler_params=_cparams(3), name="fox_attn",
    )(p, p, p, p, f_cum)


def _ret_kernel(q_ref, k_ref, v_ref, g_ref, cos_ref, sin_ref, lg_ref, o_ref, r_ref, *, tc):
    c = RET_CHUNK

    @pl.when(pl.program_id(2) == 0)
    def _():
        r_ref[...] = jnp.zeros_like(r_ref)

    lg = lg_ref[0:1, 0:1]
    rowi = lax.broadcasted_iota(jnp.int32, (c, c), 0)
    coli = lax.broadcasted_iota(jnp.int32, (c, c), 1)
    diff = (rowi - coli).astype(F32)
    decay = jnp.where(diff >= 0, jnp.exp(lg * jnp.maximum(diff, 0.0)), 0.0)
    pos = lax.broadcasted_iota(jnp.int32, (c, 1), 0).astype(F32)
    q_scale = jnp.exp(lg * (pos + 1.0))
    k_scale = jnp.exp(lg * (c - 1.0 - pos))
    chunk_decay = jnp.exp(lg * c)

    def rope(t, cos2, sin2):
        return t * cos2 + pltpu.roll(t, HEAD_DIM // 2, 1) * sin2

    for ci in range(tc // c):
        sl = slice(ci * c, (ci + 1) * c)
        cos2 = cos_ref[sl, :]
        sin2 = sin_ref[sl, :]
        q = rope(q_ref[sl, :].astype(F32), cos2, sin2)
        k = rope(k_ref[sl, :].astype(F32), cos2, sin2) * (HEAD_DIM ** -0.5)
        v = v_ref[sl, :]
        inner = _dot_nt(q.astype(BF16), k.astype(BF16)) * decay
        r_prev = r_ref[...]
        o = _dot(inner.astype(BF16), v) + _dot((q * q_scale).astype(BF16), r_prev.astype(BF16))
        r_ref[...] = r_prev * chunk_decay + _dot_tn((k * k_scale).astype(BF16), v)
        o = _rms_rows(o) * _silu(g_ref[sl, :].astype(F32))
        o_ref[sl, :] = o.astype(o_ref.dtype)


def retention_mix(p, cos2, sin2, lg_tab, batch, seq, n_heads, q_blk, k_blk, v_blk, g_blk, tc=512):
    t = p.shape[0]
    tc = min(tc, seq)
    ns = seq // tc
    kern = functools.partial(_ret_kernel, tc=tc)
    return pl.pallas_call(
        kern,
        out_shape=jax.ShapeDtypeStruct((t, n_heads * HEAD_DIM), BF16),
        grid=(batch, n_heads, ns),
        in_specs=[pl.BlockSpec((tc, HEAD_DIM), lambda b, h, s: (b * ns + s, q_blk + h)),
                  pl.BlockSpec((tc, HEAD_DIM), lambda b, h, s: (b * ns + s, k_blk + h)),
                  pl.BlockSpec((tc, HEAD_DIM), lambda b, h, s: (b * ns + s, v_blk + h)),
                  pl.BlockSpec((tc, HEAD_DIM), lambda b, h, s: (b * ns + s, g_blk + h)),
                  pl.BlockSpec((tc, HEAD_DIM), lambda b, h, s: (s, 0)),
                  pl.BlockSpec((tc, HEAD_DIM), lambda b, h, s: (s, 0)),
                  pl.BlockSpec((None, 8, LANES), lambda b, h, s: (h, 0, 0))],
        out_specs=pl.BlockSpec((tc, HEAD_DIM), lambda b, h, s: (b * ns + s, h)),
        scratch_shapes=[pltpu.VMEM((HEAD_DIM, HEAD_DIM), F32)],
        compiler_params=_cparams(3), name="retention",
    )(p, p, p, p, cos2, sin2, lg_tab)


def _dn_kernel(uq_ref, uk_ref, uv_ref, z_ref, cwq_ref, cwk_ref, cwv_ref, sm_ref, smt_ref,
               crow_ref, ccol_ref, ng_ref, o_ref,
               state_ref, bq_ref, bk_ref, bv_ref, qs_ref, ks_ref, vs_ref, *, tr):
    c = DN_CHUNK
    g_heads = DN_GROUP
    gw = g_heads * HEAD_DIM

    @pl.when(pl.program_id(2) == 0)
    def _():
        state_ref[...] = jnp.zeros_like(state_ref)
        bq_ref[0:8, :] = jnp.zeros((8, gw), F32)
        bk_ref[0:8, :] = jnp.zeros((8, gw), F32)
        bv_ref[0:8, :] = jnp.zeros((8, gw), F32)

    def conv(u_ref, cw_ref, buf_ref, dst_ref):
        buf_ref[8:8 + tr, :] = u_ref[...].astype(F32)
        y = cw_ref[3:4, :] * buf_ref[8:8 + tr, :]
        for j in range(CONV_K - 1):
            y = y + cw_ref[j:j + 1, :] * buf_ref[5 + j:5 + j + tr, :]
        buf_ref[0:8, :] = buf_ref[tr:tr + 8, :]
        dst_ref[...] = _silu(y)

    conv(uq_ref, cwq_ref, bq_ref, qs_ref)
    conv(uk_ref, cwk_ref, bk_ref, ks_ref)
    conv(uv_ref, cwv_ref, bv_ref, vs_ref)

    sm = sm_ref[...]
    beta_all = _sigmoid(sm)
    g_all = crow_ref[1:2, :] * _softplus(sm + crow_ref[0:1, :])
    smt = smt_ref[...]
    g_rows = ccol_ref[1, :, 0:1] * _softplus(smt + ccol_ref[0, :, 0:1])

    rowi = lax.broadcasted_iota(jnp.int32, (c, c), 0)
    coli = lax.broadcasted_iota(jnp.int32, (c, c), 1)
    incl = rowi >= coli
    strict = rowi > coli
    tri_l = jnp.where(incl, 1.0, 0.0).astype(BF16)
    tri_u = jnp.where(rowi <= coli, 1.0, 0.0).astype(BF16)
    ng = ng_ref[...]
    n_chunks = tr // c
    items = [(ci, hh) for ci in range(n_chunks) for hh in range(g_heads)]

    gc_cols, gc_rowsl = [], []
    for ci in range(n_chunks):
        sl = slice(ci * c, (ci + 1) * c)
        gh, gm, gl = _split3(g_all[sl, :])
        gc_cols.append(_dot(tri_l, gh) + _dot(tri_l, gm) + _dot(tri_l, gl))
        rh, rm, rl = _split3(g_rows[:, sl])
        gc_rowsl.append(_dot(rh, tri_u) + _dot(rm, tri_u) + _dot(rl, tri_u))

    qn, kn, vv, beta, gcc, gamma, qk, xm = {}, {}, {}, {}, {}, {}, {}, {}
    for it in items:
        ci, hh = it
        sl = slice(ci * c, (ci + 1) * c)
        hs = slice(hh * HEAD_DIM, (hh + 1) * HEAD_DIM)
        q = qs_ref[sl, hs]
        k = ks_ref[sl, hs]
        vv[it] = vs_ref[sl, hs]
        q = q * lax.rsqrt(jnp.sum(q * q, axis=-1, keepdims=True) + EPS) * (HEAD_DIM ** -0.5)
        k = k * lax.rsqrt(jnp.sum(k * k, axis=-1, keepdims=True) + EPS)
        qn[it], kn[it] = q, k
        beta[it] = beta_all[sl, hh:hh + 1]
        gcc[it] = gc_cols[ci][:, g_heads + hh:g_heads + hh + 1]
        gc_r = gc_rowsl[ci][g_heads + hh:g_heads + hh + 1, :]
        gamma[it] = jnp.exp(jnp.where(incl, gcc[it] - gc_r, -jnp.inf))
        kb = k.astype(BF16)
        qkk = _dot_nt(jnp.concatenate([q.astype(BF16), kb], axis=0), kb)
        qk[it] = qkk[:c]
        xm[it] = -jnp.where(strict, beta[it] * qkk[c:] * gamma[it], 0.0)

    n_lvls = int(np.log2(c)) - 3
    same8 = (rowi // 8) == (coli // 8)
    nm = {}
    for it in items:
        x8 = jnp.where(same8, xm[it], 0.0)
        xb = x8.astype(BF16)
        p1 = _dot(xb, xb)
        r = _dot(jnp.concatenate([x8, p1], axis=0).astype(BF16), p1.astype(BF16))
        n3 = x8 + p1 + r[:c]
        p2 = r[c:]
        nm[it] = n3 + p2 + _dot(n3.astype(BF16), p2.astype(BF16))
    for lvl in range(n_lvls):
        inner = (rowi // (8 << lvl)) == (coli // (8 << lvl))
        outer = (rowi // (16 << lvl)) == (coli // (16 << lvl))
        emask = outer & jnp.logical_not(inner)
        for it in items:
            e = jnp.where(emask, -xm[it], 0.0)
            m1 = e + _dot(nm[it].astype(BF16), e.astype(BF16))
            ded = m1 + _dot(m1.astype(BF16), nm[it].astype(BF16))
            nm[it] = nm[it] - ded

    q_eff, o_0, d_mat, c_mat, g_last = {}, {}, {}, {}, {}
    for it in items:
        e_gc = jnp.exp(gcc[it])
        rhs = jnp.concatenate([kn[it] * (beta[it] * e_gc), vv[it] * beta[it]], axis=1)
        solb = (rhs + _dot(nm[it].astype(BF16), rhs.astype(BF16))).astype(BF16)
        attn = (qk[it] * gamma[it]).astype(BF16)
        aw = _dot(attn, solb)
        gc_last = gcc[it][c - 1:c, :]
        k_dec = (kn[it] * jnp.exp(gc_last - gcc[it])).astype(BF16)
        kw = _dot_tn(k_dec, solb)
        q_eff[it] = (qn[it] * e_gc - aw[:, :HEAD_DIM]).astype(BF16)
        o_0[it] = aw[:, HEAD_DIM:]
        d_mat[it] = kw[:, :HEAD_DIM].astype(BF16)
        c_mat[it] = kw[:, HEAD_DIM:]
        g_last[it] = jnp.exp(gc_last)

    states = [state_ref[hh] for hh in range(g_heads)]
    for it in items:
        ci, hh = it
        sl = slice(ci * c, (ci + 1) * c)
        hs = slice(hh * HEAD_DIM, (hh + 1) * HEAD_DIM)
        st = states[hh]
        stb = st.astype(BF16)
        o = _dot(q_eff[it], stb) + o_0[it]
        states[hh] = st * g_last[it] + (c_mat[it] - _dot(d_mat[it], stb))
        o = _rms_rows(o) * ng * _silu(z_ref[sl, hs].astype(F32))
        o_ref[sl, hs] = o.astype(o_ref.dtype)
    for hh in range(g_heads):
        state_ref[hh] = states[hh]


def deltanet_mix(p, conv_w8, small, small_t, crow, ccol, norm_g, batch, seq, n_heads,
                 q_blk, k_blk, v_blk, z_blk, tr=512):
    t = p.shape[0]
    tr = min(tr, seq)
    ns = seq // tr
    ngroups = n_heads // DN_GROUP
    gw = DN_GROUP * HEAD_DIM
    kern = functools.partial(_dn_kernel, tr=tr)
    row = lambda b, g, s: b * ns + s
    return pl.pallas_call(
        kern,
        out_shape=jax.ShapeDtypeStruct((t, n_heads * HEAD_DIM), BF16),
        grid=(batch, ngroups, ns),
        in_specs=[pl.BlockSpec((tr, gw), lambda b, g, s: (row(b, g, s), q_blk + g)),
                  pl.BlockSpec((tr, gw), lambda b, g, s: (row(b, g, s), k_blk + g)),
                  pl.BlockSpec((tr, gw), lambda b, g, s: (row(b, g, s), v_blk + g)),
                  pl.BlockSpec((tr, gw), lambda b, g, s: (row(b, g, s), z_blk + g)),
                  pl.BlockSpec((8, gw), lambda b, g, s: (0, q_blk + g)),
                  pl.BlockSpec((8, gw), lambda b, g, s: (0, k_blk + g)),
                  pl.BlockSpec((8, gw), lambda b, g, s: (0, v_blk + g)),
                  pl.BlockSpec((tr, LANES), lambda b, g, s: (row(b, g, s), g)),
                  pl.BlockSpec((None, None, 8, tr), lambda b, g, s: (b, g, 0, s)),
                  pl.BlockSpec((None, 8, LANES), lambda b, g, s: (g, 0, 0)),
                  pl.BlockSpec((None, 2, 8, LANES), lambda b, g, s: (g, 0, 0, 0)),
                  pl.BlockSpec((1, HEAD_DIM), lambda b, g, s: (0, 0))],
        out_specs=pl.BlockSpec((tr, gw), lambda b, g, s: (row(b, g, s), g)),
        scratch_shapes=[pltpu.VMEM((DN_GROUP, HEAD_DIM, HEAD_DIM), F32),
                        pltpu.VMEM((tr + 8, gw), F32), pltpu.VMEM((tr + 8, gw), F32),
                        pltpu.VMEM((tr + 8, gw), F32),
                        pltpu.VMEM((tr, gw), F32), pltpu.VMEM((tr, gw), F32), pltpu.VMEM((tr, gw), F32)],
        compiler_params=_cparams(3), name="deltanet",
    )(p, p, p, p, conv_w8, conv_w8, conv_w8, small, small_t, crow, ccol, norm_g)


def _norm_router_kernel(x_ref, g_ref, sc_ref, sh_ref, wr_ref, br_ref, h_ref, lg_ref):
    x = x_ref[...]
    h = _rms_rows(x) * g_ref[...] * (1.0 + sc_ref[0]) + sh_ref[0]
    half = h.shape[1] // 2
    h_ref[...] = _bf16_bits_lo(h[:, :half]) | _bf16_bits_hi(h[:, half:])
    w = wr_ref[...]
    w_hi = w.astype(BF16)
    w_lo = (w - w_hi.astype(F32)).astype(BF16)
    h_hi = h.astype(BF16)
    h_lo = (h - h_hi.astype(F32)).astype(BF16)
    lg_ref[...] = _dot(h_hi, w_hi) + _dot(h_lo, w_hi) + _dot(h_hi, w_lo) + br_ref[...]


def norm_router(x2d, g, sc, sh, w_router_pad, b_router_pad, seq, tm=256):
    t, d = x2d.shape
    tm = min(tm, seq)
    nb = seq // tm
    return pl.pallas_call(
        _norm_router_kernel,
        out_shape=(jax.ShapeDtypeStruct((t, d // 2), U32), jax.ShapeDtypeStruct((t, LANES), F32)),
        grid=(t // tm,),
        in_specs=[pl.BlockSpec((tm, d), lambda i: (i, 0)),
                  pl.BlockSpec((1, d), lambda i: (0, 0)),
                  pl.BlockSpec((1, 1, d), lambda i: (i // nb, 0, 0)),
                  pl.BlockSpec((1, 1, d), lambda i: (i // nb, 0, 0)),
                  pl.BlockSpec((d, LANES), lambda i: (0, 0)),
                  pl.BlockSpec((1, LANES), lambda i: (0, 0))],
        out_specs=(pl.BlockSpec((tm, d // 2), lambda i: (i, 0)),
                   pl.BlockSpec((tm, LANES), lambda i: (i, 0))),
        compiler_params=_cparams(1), name="norm_router",
    )(x2d, g.reshape(1, d), sc, sh, w_router_pad, b_router_pad)


def _moe_kernel(blk_e_ref, first_ref, nxt_ref, nused_ref, tok_ref, tokn_ref, h_hbm, wgu_hbm, wd_hbm,
                bgu_ref, bd_ref, y_ref,
                stg_gu, stg_d, wgu_b, wd_b, perm_ref, xbuf0, xbuf1, wsem, gsem, *, tb, layer):
    i = pl.program_id(0)
    nused = nused_ref[0]
    d, f2 = stg_gu.shape
    f = f2 // 2
    xbufs = (xbuf0, xbuf1)

    def weight_copies(e):
        return (pltpu.make_async_copy(wgu_hbm.at[layer, e], stg_gu, wsem.at[0]),
                pltpu.make_async_copy(wd_hbm.at[layer, e], stg_d, wsem.at[1]))

    def row_copy(toks, r, s):
        return pltpu.make_async_copy(h_hbm.at[toks[0, r]],
                                     xbufs[s].at[pl.ds(r, 1), :], gsem.at[s])

    def wait_gather(s):
        pltpu.make_async_copy(xbufs[s], xbufs[s], gsem.at[s]).wait()

    @pl.when(i == 0)
    def _():
        for cp in weight_copies(blk_e_ref[0]):
            cp.start(priority=1)

        def issue(r, carry):
            row_copy(tok_ref, r, 0).start()
            return carry
        lax.fori_loop(0, tb, issue, 0)
        rr = lax.broadcasted_iota(jnp.int32, (f2, f), 0)
        cc = lax.broadcasted_iota(jnp.int32, (f2, f), 1)
        perm_ref[...] = jnp.where(rr == 2 * cc, 1.0, 0.0).astype(BF16)

    @pl.when((i < nused) & (first_ref[i] == 1))
    def _():
        for cp in weight_copies(0):
            cp.wait()
        rows = min(256, f)

        def cast_gu(r, carry):
            r0 = pl.multiple_of(r * rows, rows)
            wgu_b[pl.ds(r0, rows), :] = stg_gu[pl.ds(r0, rows), :].astype(BF16)
            return carry
        lax.fori_loop(0, d // rows, cast_gu, 0)

        def cast_d(r, carry):
            r0 = pl.multiple_of(r * rows, rows)
            wd_b[pl.ds(r0, rows), :] = stg_d[pl.ds(r0, rows), :].astype(BF16)
            return carry
        lax.fori_loop(0, f // rows, cast_d, 0)

        @pl.when(nxt_ref[i] >= 0)
        def _():
            for cp in weight_copies(nxt_ref[i]):
                cp.start(priority=1)

    def compute(s):
        wait_gather(s)
        xw = xbufs[s][...]
        x = jnp.concatenate([pltpu.bitcast(xw << 16, F32).astype(BF16),
                             pltpu.bitcast(xw & jnp.uint32(HI16), F32).astype(BF16)], axis=1)
        for r in range(tb):
            row_copy(tokn_ref, r, 1 - s).start()
        gu = _dot(x, wgu_b[...]) + bgu_ref[...]
        gate = jnp.minimum(gu, SWIGLU_LIMIT)
        sg = gate * _sigmoid(SWIGLU_ALPHA * gate)
        lin1 = jnp.clip(gu, -SWIGLU_LIMIT, SWIGLU_LIMIT) + 1.0
        act_il = sg * pltpu.roll(lin1, f2 - 1, 1)
        act = _dot(act_il.astype(BF16), perm_ref[...])
        y = _dot(act.astype(BF16), wd_b[...]) + bd_ref[...]
        y_ref[...] = _bf16_bits_lo(y[:, :d // 2]) | _bf16_bits_hi(y[:, d // 2:])

    for s in range(2):
        @pl.when((i < nused) & (i % 2 == s))
        def _(s=s):
            compute(s)

        @pl.when((i == nused) & (i % 2 == s))
        def _(s=s):
            wait_gather(s)

    @pl.when(i >= nused)
    def _():
        y_ref[...] = jnp.zeros_like(y_ref)


def moe_experts(h2, w_gu, w_down, b_gu_l, b_down_l, layer, blk_e, first, nxt, n_used, row_tok, tb):
    d = w_gu.shape[2]
    n_blk = row_tok.shape[0]
    f2 = w_gu.shape[3]
    f = f2 // 2
    kern = functools.partial(_moe_kernel, tb=tb, layer=layer)
    grid_spec = pltpu.PrefetchScalarGridSpec(
        num_scalar_prefetch=4,
        grid=(n_blk,),
        in_specs=[pl.BlockSpec((None, 1, tb), lambda i, *_: (i, 0, 0), memory_space=pltpu.SMEM),
                  pl.BlockSpec((None, 1, tb), lambda i, *_: (jnp.minimum(i + 1, n_blk - 1), 0, 0),
                               memory_space=pltpu.SMEM),
                  pl.BlockSpec(memory_space=pl.ANY),
                  pl.BlockSpec(memory_space=pl.ANY),
                  pl.BlockSpec(memory_space=pl.ANY),
                  pl.BlockSpec((None, 1, f2), lambda i, be, *_: (be[i], 0, 0)),
                  pl.BlockSpec((None, 1, d), lambda i, be, *_: (be[i], 0, 0))],
        out_specs=pl.BlockSpec((tb, d // 2), lambda i, *_: (i, 0)),
        scratch_shapes=[pltpu.VMEM((d, f2), F32), pltpu.VMEM((f, d), F32),
                        pltpu.VMEM((d, f2), BF16), pltpu.VMEM((f, d), BF16),
                        pltpu.VMEM((f2, f), BF16), pltpu.VMEM((tb, d // 2), U32),
                        pltpu.VMEM((tb, d // 2), U32),
                        pltpu.SemaphoreType.DMA((2,)), pltpu.SemaphoreType.DMA((2,))])
    return pl.pallas_call(
        kern,
        out_shape=jax.ShapeDtypeStruct((n_blk * tb, d // 2), U32),
        grid_spec=grid_spec,
        compiler_params=_cparams(1, MOE_VMEM_LIMIT), name="moe_experts",
    )(blk_e, first, nxt, n_used, row_tok, row_tok, h2, w_gu, w_down, b_gu_l, b_down_l)


def _combine_kernel(pos_ref, posn_ref, y_hbm, x_ref, tw_ref, g2_ref, ng_ref, sc_ref, sh_ref, *rest,
                    tm, final, n_steps):
    if final:
        out_ref, ybuf0, ybuf1, sem = rest
    else:
        xo_ref, h_ref, ybuf0, ybuf1, sem = rest
    i = pl.program_id(0)
    ybufs = (ybuf0, ybuf1)

    def start_gather(p_ref, s):
        def issue(r, carry):
            for k in range(TOP_K):
                pltpu.make_async_copy(y_hbm.at[pl.ds(p_ref[0, r * TOP_K + k], 1), :],
                                      ybufs[s].at[k, pl.ds(r, 1), :], sem.at[s]).start()
            return carry
        lax.fori_loop(0, tm, issue, 0)

    def wait_gather(s):
        for k in range(TOP_K):
            pltpu.make_async_copy(y_hbm.at[pl.ds(0, tm), :], ybufs[s].at[k], sem.at[s]).wait()

    def lo(w):
        return pltpu.bitcast(w << 16, F32)

    def hi(w):
        return pltpu.bitcast(w & jnp.uint32(HI16), F32)

    def finish(s):
        tw = tw_ref[...]
        acc_lo = acc_hi = None
        for k in range(TOP_K):
            w = ybufs[s][k]
            wk = tw[:, k:k + 1]
            acc_lo = wk * lo(w) if acc_lo is None else acc_lo + wk * lo(w)
            acc_hi = wk * hi(w) if acc_hi is None else acc_hi + wk * hi(w)
        x = x_ref[...] + g2_ref[0] * jnp.concatenate([acc_lo, acc_hi], axis=1)
        if final:
            out_ref[...] = _rms_rows(x) * ng_ref[...]
        else:
            xo_ref[...] = x
            h = _rms_rows(x) * ng_ref[...] * (1.0 + sc_ref[0]) + sh_ref[0]
            h_ref[...] = h.astype(h_ref.dtype)

    @pl.when(i == 0)
    def _():
        start_gather(pos_ref, 0)

    for s in range(2):
        @pl.when(i % 2 == s)
        def _(s=s):
            @pl.when(i + 1 < n_steps)
            def _():
                start_gather(posn_ref, 1 - s)
            wait_gather(s)
            finish(s)


def moe_combine(y, pos, top_w_pad, x2d, gate2, norm_g, sc, sh, seq, final, tm=128):
    t, d = x2d.shape
    tm = min(tm, seq)
    nb = seq // tm
    n_steps = t // tm
    kern = functools.partial(_combine_kernel, tm=tm, final=final, n_steps=n_steps)
    if final:
        out_shape = jax.ShapeDtypeStruct((t, d), F32)
        out_specs = pl.BlockSpec((tm, d), lambda i: (i, 0))
    else:
        out_shape = (jax.ShapeDtypeStruct((t, d), F32), jax.ShapeDtypeStruct((t, d), BF16))
        out_specs = (pl.BlockSpec((tm, d), lambda i: (i, 0)), pl.BlockSpec((tm, d), lambda i: (i, 0)))
    pos3 = pos.reshape(n_steps, 1, tm * TOP_K)
    return pl.pallas_call(
        kern,
        out_shape=out_shape,
        grid=(n_steps,),
        in_specs=[pl.BlockSpec((None, 1, tm * TOP_K), lambda i: (i, 0, 0), memory_space=pltpu.SMEM),
                  pl.BlockSpec((None, 1, tm * TOP_K), lambda i: (jnp.minimum(i + 1, n_steps - 1), 0, 0),
                               memory_space=pltpu.SMEM),
                  pl.BlockSpec(memory_space=pl.ANY),
                  pl.BlockSpec((tm, d), lambda i: (i, 0)),
                  pl.BlockSpec((tm, LANES), lambda i: (i, 0)),
                  pl.BlockSpec((1, 1, d), lambda i: (i // nb, 0, 0)),
                  pl.BlockSpec((1, d), lambda i: (0, 0)),
                  pl.BlockSpec((1, 1, d), lambda i: (i // nb, 0, 0)),
                  pl.BlockSpec((1, 1, d), lambda i: (i // nb, 0, 0))],
        out_specs=out_specs,
        scratch_shapes=[pltpu.VMEM((TOP_K, tm, d // 2), U32), pltpu.VMEM((TOP_K, tm, d // 2), U32),
                        pltpu.SemaphoreType.DMA((2,))],
        compiler_params=_cparams(1), name="moe_combine",
    )(pos3, pos3, y, x2d, top_w_pad, gate2, norm_g.reshape(1, d), sc, sh)


def _routing_tables(logits, n_experts, tb):
    t = logits.shape[0]
    top_v, top_e = lax.top_k(logits[:, :n_experts], TOP_K)
    top_w = jax.nn.softmax(top_v, axis=-1)
    n_a = t * TOP_K
    eids = jnp.arange(n_experts, dtype=jnp.int32)
    sel = jnp.any(top_e[:, :, None] == eids[None, None, :], axis=1).astype(jnp.int32)
    csum = jnp.cumsum(sel, axis=0)
    counts = csum[-1]
    rank = jnp.take_along_axis(csum - sel, top_e, axis=1)
    padded = (counts + tb - 1) // tb * tb
    pend = jnp.cumsum(padded)
    pstart = pend - padded
    pos = (pstart[top_e] + rank).astype(jnp.int32).reshape(n_a)
    n_rows = (n_a + n_experts * (tb - 1) + tb - 1) // tb * tb + tb
    n_blk = n_rows // tb
    t_a = jnp.repeat(jnp.arange(t, dtype=jnp.int32), TOP_K)
    row_tok = jnp.zeros((n_rows,), jnp.int32).at[pos].set(t_a)
    blk_start = jnp.arange(n_blk, dtype=jnp.int32) * tb
    blk_e = jnp.minimum(jnp.sum((pend[None, :] <= blk_start[:, None]).astype(jnp.int32), axis=1),
                        n_experts - 1).astype(jnp.int32)
    n_used = (pend[-1] // tb).astype(jnp.int32)
    used = jnp.arange(n_blk) < n_used
    first = (used & (blk_start == pstart[blk_e])).astype(jnp.int32)
    has = counts > 0
    cand = jnp.where(has, eids, n_experts)
    suffix_min = lax.cummin(cand, axis=0, reverse=True)
    nxt_e = jnp.concatenate([suffix_min[1:], jnp.full((1,), n_experts, jnp.int32)])
    nxt_e = jnp.where(nxt_e >= n_experts, -1, nxt_e).astype(jnp.int32)
    nxt = nxt_e[blk_e]
    top_w_pad = jnp.pad(top_w, ((0, 0), (0, LANES - TOP_K)))
    return pos, top_w_pad, row_tok.reshape(n_blk, 1, tb), blk_e, first, nxt, n_used.reshape(1)


def _rope_tables(seq):
    half = HEAD_DIM // 2
    inv = ROPE_BASE ** (-jnp.arange(half, dtype=F32) / half)
    ang = jnp.arange(seq, dtype=F32)[:, None] * inv[None, :]
    cos, sin = jnp.cos(ang), jnp.sin(ang)
    return jnp.concatenate([cos, cos], axis=-1), jnp.concatenate([-sin, sin], axis=-1)


def kernel(x, c, norm1_g, w_ada, b_ada, w_in, dn_conv_w, dn_a_log, dn_dt_bias, dn_norm_g, fox_f_bias,
           w_out, norm2_g, w_router, b_router, w_gu, b_gu, w_down, b_down, final_norm_g):
    batch, seq, d = x.shape
    depth = w_ada.shape[0]
    t = batch * seq
    dn_heads = dn_a_log.shape[1]
    fox_heads = fox_f_bias.shape[1]
    ret_heads = d // HEAD_DIM - dn_heads - fox_heads
    dn_w, ret_w, fox_w = dn_heads * HEAD_DIM, ret_heads * HEAD_DIM, fox_heads * HEAD_DIM
    n_experts = w_router.shape[2]
    n_groups = dn_heads // DN_GROUP
    gw = DN_GROUP * HEAD_DIM
    moe_tb = 256
    tn = 512

    widths = (3 * dn_w, dn_w, dn_heads, dn_heads, ret_w, ret_w, ret_w, ret_w,
              fox_w, fox_w, fox_w, fox_w, fox_heads)
    cuts = np.concatenate([[0], np.cumsum(widths)])
    o_b, o_a, o_r, o_ff = int(cuts[2]), int(cuts[3]), int(cuts[4]), int(cuts[12])
    n_main = o_b + (o_ff - o_r)
    small_src, small_dst = [], []
    for g in range(n_groups):
        for hh in range(DN_GROUP):
            small_src += [g * DN_GROUP + hh, dn_heads + g * DN_GROUP + hh]
            small_dst += [g * LANES + hh, g * LANES + DN_GROUP + hh]
    for hh in range(fox_heads):
        small_src.append(2 * dn_heads + hh)
        small_dst.append(n_groups * LANES + hh)
    small_src = np.asarray(small_src, np.int32)
    small_dst = np.asarray(small_dst, np.int32)
    w_in_t = jnp.transpose(w_in, (2, 0, 1))
    w_main_t = w_in_prep(w_in_t, o_b, o_r - o_b, n_main)
    w_sc = jnp.concatenate([w_in_t[o_b:o_r], w_in_t[o_ff:]], axis=0)
    w_small_t = jnp.zeros((depth, (n_groups + 1) * LANES, d), BF16).at[:, small_dst].set(
        jnp.transpose(w_sc, (1, 0, 2))[:, small_src].astype(BF16))

    x2d = x.reshape(t, d)
    c_pad = jnp.zeros((8, d), BF16).at[:batch].set(c.astype(BF16))
    mod = ada_mod(c_pad, w_ada, b_ada)[:, :batch, :]
    mod = mod.reshape(depth, batch, 6, 1, d)
    cos2, sin2 = _rope_tables(seq)
    lg_tab = jnp.log(1.0 - 2.0 ** (-5.0 - jnp.arange(ret_heads, dtype=F32)))
    lg_tab = jnp.broadcast_to(lg_tab[:, None, None], (ret_heads, 8, LANES))

    dn_q_blk, dn_k_blk, dn_v_blk, dn_z_blk = 0, dn_w // gw, 2 * dn_w // gw, 3 * dn_w // gw
    r0 = o_b // HEAD_DIM
    r_q_blk, r_k_blk, r_v_blk, r_g_blk = r0, r0 + ret_heads, r0 + 2 * ret_heads, r0 + 3 * ret_heads
    f0 = r0 + 4 * ret_heads
    f_q_blk, f_k_blk, f_v_blk, f_z_blk = f0, f0 + fox_heads, f0 + 2 * fox_heads, f0 + 3 * fox_heads

    h = norm_mod(x2d, norm1_g[0], mod[0, :, 1], mod[0, :, 0], seq)
    out = None
    for l in range(depth):
        sh1, sc1, g1, sh2, sc2, g2 = (mod[l, :, i] for i in range(6))
        p = matmul_nt(h, w_main_t, l, BF16, "in_proj")
        small = matmul_nt(h, w_small_t, l, F32, "small_proj")

        conv_w8 = jnp.zeros((8, 3 * dn_w), F32).at[:CONV_K].set(dn_conv_w[l])
        small_t = small[:, :n_groups * LANES].reshape(batch, seq, n_groups, LANES)[..., :8]
        small_t = small_t.transpose(0, 2, 3, 1)
        dtb = dn_dt_bias[l].reshape(n_groups, DN_GROUP)
        nega = -jnp.exp(dn_a_log[l]).reshape(n_groups, DN_GROUP)
        zg = jnp.zeros((n_groups, DN_GROUP), F32)
        pad = jnp.zeros((n_groups, LANES - 2 * DN_GROUP), F32)
        crow = jnp.stack([jnp.concatenate([zg, dtb, pad], axis=1),
                          jnp.concatenate([zg, nega, pad], axis=1)], axis=1)
        crow = jnp.concatenate([crow, jnp.zeros((n_groups, 6, LANES), F32)], axis=1)
        ccol = jnp.stack([jnp.concatenate([zg, dtb], axis=1), jnp.concatenate([zg, nega], axis=1)], axis=1)
        ccol = jnp.broadcast_to(ccol[..., None], (n_groups, 2, 8, LANES))
        o_dn = deltanet_mix(p, conv_w8, small, small_t, crow, ccol, dn_norm_g[l].reshape(1, HEAD_DIM),
                            batch, seq, dn_heads, dn_q_blk, dn_k_blk, dn_v_blk, dn_z_blk)
        o_ret = retention_mix(p, cos2, sin2, lg_tab, batch, seq, ret_heads,
                              r_q_blk, r_k_blk, r_v_blk, r_g_blk)
        fb = jnp.zeros((1, LANES), F32).at[0, :fox_heads].set(fox_f_bias[l])
        f_cum = fox_prefix(small, fb, batch, seq, n_groups)
        o_fox = fox_attention(p, f_cum, batch, seq, fox_heads, f_q_blk, f_k_blk, f_v_blk, f_z_blk)
        x2d = out_proj_resid(o_dn, o_ret, o_fox, w_out, l, x2d, g1, seq)

        wr_pad = jnp.zeros((d, LANES), F32).at[:, :n_experts].set(w_router[l])
        br_pad = jnp.zeros((1, LANES), F32).at[0, :n_experts].set(b_router[l])
        h2, logits = norm_router(x2d, norm2_g[l], sc2, sh2, wr_pad, br_pad, seq)
        pos, top_w_pad, row_tok, blk_e, first, nxt, n_used = _routing_tables(logits, n_experts, moe_tb)
        y = moe_experts(h2.reshape(t, 1, d // 2), w_gu, w_down, b_gu[l][:, None, :], b_down[l][:, None, :], l,
                        blk_e, first, nxt, n_used, row_tok, moe_tb)
        if l + 1 < depth:
            nsh1, nsc1 = mod[l + 1, :, 0], mod[l + 1, :, 1]
            x2d, h = moe_combine(y, pos, top_w_pad, x2d, g2, norm1_g[l + 1], nsc1, nsh1, seq, final=False)
        else:
            out = moe_combine(y, pos, top_w_pad, x2d, g2, final_norm_g, sc2, sh2, seq, final=True)
    return out.reshape(batch, seq, d)
```

```python
import functools

import numpy as np
import jax
import jax.numpy as jnp
from jax import lax
from jax.experimental import pallas as pl
from jax.experimental.pallas import tpu as pltpu

F32 = jnp.float32
BF16 = jnp.bfloat16
U32 = jnp.uint32

HEAD_DIM = 128
CONV_K = 4
DN_CHUNK = 128
RET_CHUNK = 128
ROPE_BASE = 10000.0
TOP_K = 4
SWIGLU_LIMIT = 7.0
SWIGLU_ALPHA = 1.702
EPS = 1e-6
DN_GROUP = 4
LANES = 128
VMEM_LIMIT = 56 * 1024 * 1024
MOE_VMEM_LIMIT = 60 * 1024 * 1024
LOG2E = 1.4426950408889634
HI16 = 0xFFFF0000


def _cparams(n_axes, vmem_limit=VMEM_LIMIT):
    return pltpu.CompilerParams(dimension_semantics=("arbitrary",) * n_axes,
                                vmem_limit_bytes=vmem_limit)


def _dot(a, b):
    return jnp.dot(a, b, preferred_element_type=F32)


def _dot_nt(a, b):
    return lax.dot_general(a, b, (((1,), (1,)), ((), ())), preferred_element_type=F32)


def _dot_tn(a, b):
    return lax.dot_general(a, b, (((0,), (0,)), ((), ())), preferred_element_type=F32)


def _split3(x):
    h = x.astype(BF16)
    r = x - h.astype(F32)
    m = r.astype(BF16)
    l = (r - m.astype(F32)).astype(BF16)
    return h, m, l


def _sigmoid(x):
    return 1.0 / (1.0 + jnp.exp(-x))


def _silu(x):
    return x * _sigmoid(x)


def _softplus(x):
    return jnp.maximum(x, 0.0) + jnp.log1p(jnp.exp(-jnp.abs(x)))


def _log_sigmoid(x):
    return jnp.minimum(x, 0.0) - jnp.log1p(jnp.exp(-jnp.abs(x)))


def _rms_rows(x):
    return x * lax.rsqrt(jnp.mean(x * x, axis=-1, keepdims=True) + EPS)


def _bf16_bits_lo(x):
    return pltpu.bitcast(x.astype(BF16).astype(F32), U32) >> 16


def _bf16_bits_hi(x):
    return pltpu.bitcast(x.astype(BF16).astype(F32), U32) & jnp.uint32(HI16)


def _ada_kernel(c_ref, w_ref, b_ref, o_ref):
    o_ref[...] = _dot(c_ref[...], w_ref[...].astype(BF16)) + b_ref[...]


def ada_mod(c_pad, w_ada, b_ada, tn=512):
    depth, d, n = w_ada.shape
    return pl.pallas_call(
        _ada_kernel,
        out_shape=jax.ShapeDtypeStruct((depth, 8, n), F32),
        grid=(depth, n // tn),
        in_specs=[pl.BlockSpec((8, d), lambda l, j: (0, 0)),
                  pl.BlockSpec((None, d, tn), lambda l, j: (l, 0, j)),
                  pl.BlockSpec((None, 1, tn), lambda l, j: (l, 0, j))],
        out_specs=pl.BlockSpec((None, 8, tn), lambda l, j: (l, 0, j)),
        compiler_params=_cparams(2), name="ada_mod",
    )(c_pad, w_ada, b_ada.reshape(depth, 1, n))


def _norm_mod_kernel(x_ref, g_ref, sc_ref, sh_ref, h_ref):
    x = x_ref[...]
    h = _rms_rows(x) * g_ref[...] * (1.0 + sc_ref[0]) + sh_ref[0]
    h_ref[...] = h.astype(h_ref.dtype)


def norm_mod(x2d, g, sc, sh, seq, tm=256):
    t, d = x2d.shape
    tm = min(tm, seq)
    nb = seq // tm
    return pl.pallas_call(
        _norm_mod_kernel,
        out_shape=jax.ShapeDtypeStruct((t, d), BF16),
        grid=(t // tm,),
        in_specs=[pl.BlockSpec((tm, d), lambda i: (i, 0)),
                  pl.BlockSpec((1, d), lambda i: (0, 0)),
                  pl.BlockSpec((1, 1, d), lambda i: (i // nb, 0, 0)),
                  pl.BlockSpec((1, 1, d), lambda i: (i // nb, 0, 0))],
        out_specs=pl.BlockSpec((tm, d), lambda i: (i, 0)),
        compiler_params=_cparams(1), name="norm_mod",
    )(x2d, g.reshape(1, d), sc, sh)


def _w_prep_kernel(w_hbm, o_ref, buf, sem, *, rows, gap_start, gap_len):
    r = pl.program_id(0) * rows
    r0 = r + jnp.where(r >= gap_start, gap_len, 0)
    depth = o_ref.shape[0]

    def layer_copy(l):
        return pltpu.make_async_copy(w_hbm.at[pl.ds(r0, rows), l, :], buf.at[l], sem.at[l])

    for l in range(depth):
        layer_copy(l).start()
    for l in range(depth):
        layer_copy(l).wait()
        o_ref[l] = buf[l].astype(BF16)


def w_in_prep(w_in_t, gap_start, gap_len, n_rows, rows=256):
    n, depth, k = w_in_t.shape
    rows = min(rows, gap_start)
    assert gap_start % rows == 0 and n_rows % rows == 0 and gap_len % 8 == 0

    kern = functools.partial(_w_prep_kernel, rows=rows, gap_start=gap_start, gap_len=gap_len)
    return pl.pallas_call(
        kern,
        out_shape=jax.ShapeDtypeStruct((depth, n_rows, k), BF16),
        grid=(n_rows // rows,),
        in_specs=[pl.BlockSpec(memory_space=pl.ANY)],
        out_specs=pl.BlockSpec((depth, rows, k), lambda j: (0, j, 0)),
        scratch_shapes=[pltpu.VMEM((depth, rows, k), F32), pltpu.SemaphoreType.DMA((depth,))],
        compiler_params=_cparams(1), name="w_in_prep",
    )(w_in_t)


def _mm_nt_kernel(x_ref, w_ref, o_ref):
    o_ref[...] = _dot_nt(x_ref[...], w_ref[...]).astype(o_ref.dtype)


def matmul_nt(x, w_t, layer, out_dtype, name, tm=1024, tn=512):
    m, k = x.shape
    n = w_t.shape[1]
    tm = min(tm, m)
    tn = min(tn, n)
    return pl.pallas_call(
        _mm_nt_kernel,
        out_shape=jax.ShapeDtypeStruct((m, n), out_dtype),
        grid=(n // tn, m // tm),
        in_specs=[pl.BlockSpec((tm, k), lambda j, i: (i, 0)),
                  pl.BlockSpec((None, tn, k), lambda j, i: (layer, j, 0))],
        out_specs=pl.BlockSpec((tm, tn), lambda j, i: (i, j)),
        compiler_params=_cparams(2), name=name,
    )(x, w_t)


def _mm_resid_kernel(a_ref, b_ref, c_ref, w_ref, x_ref, g_ref, o_ref, wb_ref, *, splits):
    @pl.when(pl.program_id(1) == 0)
    def _():
        wb_ref[...] = w_ref[...].astype(BF16)
    k0, k1 = splits
    y = (_dot(a_ref[...], wb_ref[0:k0, :]) + _dot(b_ref[...], wb_ref[k0:k1, :])
         + _dot(c_ref[...], wb_ref[k1:, :]))
    o_ref[...] = x_ref[...] + g_ref[0] * y


def out_proj_resid(a, b, c, w_out, layer, x2d, gate, seq, tm=1024, tn=512):
    m = a.shape[0]
    ka, kb, kc = a.shape[1], b.shape[1], c.shape[1]
    k = ka + kb + kc
    n = w_out.shape[2]
    tm = min(tm, seq)
    nb = seq // tm
    kern = functools.partial(_mm_resid_kernel, splits=(ka, ka + kb))
    return pl.pallas_call(
        kern,
        out_shape=jax.ShapeDtypeStruct((m, n), F32),
        grid=(n // tn, m // tm),
        in_specs=[pl.BlockSpec((tm, ka), lambda j, i: (i, 0)),
                  pl.BlockSpec((tm, kb), lambda j, i: (i, 0)),
                  pl.BlockSpec((tm, kc), lambda j, i: (i, 0)),
                  pl.BlockSpec((None, k, tn), lambda j, i: (layer, 0, j)),
                  pl.BlockSpec((tm, tn), lambda j, i: (i, j)),
                  pl.BlockSpec((1, 1, tn), lambda j, i: (i // nb, 0, j))],
        out_specs=pl.BlockSpec((tm, tn), lambda j, i: (i, j)),
        scratch_shapes=[pltpu.VMEM((k, tn), BF16)],
        compiler_params=_cparams(2), name="out_proj",
    )(a, b, c, w_out, x2d, gate)


def _fprep_kernel(s_ref, b_ref, f_ref, carry_ref):
    @pl.when(pl.program_id(1) == 0)
    def _():
        carry_ref[...] = jnp.zeros_like(carry_ref)
    tq = s_ref.shape[0]
    lf = _log_sigmoid(s_ref[...] + b_ref[...])
    row = lax.broadcasted_iota(jnp.int32, (tq, tq), 0)
    col = lax.broadcasted_iota(jnp.int32, (tq, tq), 1)
    tri = jnp.where(row >= col, 1.0, 0.0).astype(BF16)
    h, m, l = _split3(lf)
    f = _dot(tri, h) + _dot(tri, m) + _dot(tri, l) + carry_ref[0:1, :]
    f_ref[...] = f
    carry_ref[...] = jnp.broadcast_to(f[tq - 1:tq, :], carry_ref.shape)


def fox_prefix(small, bias_row, batch, seq, col_block, tq=512):
    t = small.shape[0]
    tq = min(tq, seq)
    nq = seq // tq
    return pl.pallas_call(
        _fprep_kernel,
        out_shape=jax.ShapeDtypeStruct((t, LANES), F32),
        grid=(batch, nq),
        in_specs=[pl.BlockSpec((tq, LANES), lambda b, i: (b * nq + i, col_block)),
                  pl.BlockSpec((1, LANES), lambda b, i: (0, 0))],
        out_specs=pl.BlockSpec((tq, LANES), lambda b, i: (b * nq + i, 0)),
        scratch_shapes=[pltpu.VMEM((8, LANES), F32)],
        compiler_params=_cparams(2), name="fox_prefix",
    )(small, bias_row)


FOX_VT_ROWS = HEAD_DIM + 16


def _fox_kernel(q_ref, k_ref, v_ref, z_ref, f_ref, o_ref, ka_ref, vt_ref, m_ref, acc_ref, *, tq, seq):
    h = pl.program_id(1)
    qi = pl.program_id(2)
    lane = lax.broadcasted_iota(jnp.int32, (tq, LANES), 1)

    def f_column(rows):
        fb = f_ref[pl.ds(rows, tq), :]
        return jnp.sum(jnp.where(lane == h, fb, 0.0), axis=-1, keepdims=True) * LOG2E

    def aug_lanes(fcol, sign_first):
        fh, fm, fl = _split3(fcol)
        fh, fm, fl = fh.astype(F32), fm.astype(F32), fl.astype(F32)
        one = jnp.ones((tq, LANES), F32)
        zero = jnp.zeros((tq, LANES), F32)
        if sign_first:
            a = jnp.where(lane == 0, fh, jnp.where(lane == 1, fm, jnp.where(lane == 2, fl,
                          jnp.where(lane < 6, one, zero))))
        else:
            a = jnp.where(lane < 3, one, jnp.where(lane == 3, -fh, jnp.where(lane == 4, -fm,
                          jnp.where(lane == 5, -fl, zero))))
        return a.astype(BF16)

    @pl.when(qi == 0)
    def _():
        def build(j, carry):
            r0 = pl.multiple_of(j * tq, tq)
            ka_ref[pl.ds(r0, tq), 0:HEAD_DIM] = k_ref[pl.ds(r0, tq), :]
            ka_ref[pl.ds(r0, tq), HEAD_DIM:] = aug_lanes(f_column(r0), False)
            vt_ref[0:HEAD_DIM, pl.ds(r0, tq)] = v_ref[pl.ds(r0, tq), :].astype(F32).T.astype(BF16)
            vt_ref[HEAD_DIM:, pl.ds(r0, tq)] = jnp.ones((FOX_VT_ROWS - HEAD_DIM, tq), BF16)
            return carry
        lax.fori_loop(0, seq // tq, build, 0)

    q0 = pl.multiple_of(qi * tq, tq)
    qs = (q_ref[...].astype(F32) * (HEAD_DIM ** -0.5 * LOG2E)).astype(BF16)
    qa = jnp.concatenate([qs, aug_lanes(f_column(q0), True)], axis=1)

    m_ref[...] = jnp.full_like(m_ref, -1e30)
    acc_ref[...] = jnp.zeros_like(acc_ref)

    def block(k0, width, masked):
        st = _dot_nt(ka_ref[pl.ds(k0, width), :], qa)
        if masked:
            key = lax.broadcasted_iota(jnp.int32, (width, tq), 0)
            qry = lax.broadcasted_iota(jnp.int32, (width, tq), 1)
            st = jnp.where(qry >= key, st, -1e30)
        m_old = m_ref[...]
        m_new = jnp.maximum(m_old, jnp.max(st, axis=0, keepdims=True))
        p = jnp.exp2(st - m_new).astype(BF16)
        acc_ref[...] = (jnp.exp2(m_old - m_new) * acc_ref[...]
                        + _dot(vt_ref[:, pl.ds(k0, width)], p))
        m_ref[...] = m_new

    def wide(j, carry):
        block(pl.multiple_of(j * (2 * tq), 2 * tq), 2 * tq, False)
        return carry
    lax.fori_loop(0, qi // 2, wide, 0)

    @pl.when(qi % 2 == 1)
    def _():
        block(pl.multiple_of((qi - 1) * tq, tq), tq, False)

    block(q0, tq, True)
    acc = acc_ref[...]
    o = (acc[:HEAD_DIM] / acc[HEAD_DIM:HEAD_DIM + 1]).T
    o_ref[...] = (o * _sigmoid(z_ref[...].astype(F32))).astype(o_ref.dtype)


def fox_attention(p, f_cum, batch, seq, n_heads, q_blk, k_blk, v_blk, z_blk, tq=512):
    t = p.shape[0]
    tq = min(tq, seq)
    nq = seq // tq
    kern = functools.partial(_fox_kernel, tq=tq, seq=seq)
    return pl.pallas_call(
        kern,
        out_shape=jax.ShapeDtypeStruct((t, n_heads * HEAD_DIM), BF16),
        grid=(batch, n_heads, nq),
        in_specs=[pl.BlockSpec((tq, HEAD_DIM), lambda b, h, i: (b * nq + i, q_blk + h)),
                  pl.BlockSpec((seq, HEAD_DIM), lambda b, h, i: (b, k_blk + h)),
                  pl.BlockSpec((seq, HEAD_DIM), lambda b, h, i: (b, v_blk + h)),
                  pl.BlockSpec((tq, HEAD_DIM), lambda b, h, i: (b * nq + i, z_blk + h)),
                  pl.BlockSpec((seq, LANES), lambda b, h, i: (b, 0))],
        out_specs=pl.BlockSpec((tq, HEAD_DIM), lambda b, h, i: (b * nq + i, h)),
        scratch_shapes=[pltpu.VMEM((seq, 2 * HEAD_DIM), BF16), pltpu.VMEM((FOX_VT_ROWS, seq), BF16),
                        pltpu.VMEM((1, tq), F32), pltpu.VMEM((FOX_VT_ROWS, tq), F32)],
        compiler_params=_cparams(3), name="fox_attn",
    )(p, p, p, p, f_cum)


def _ret_kernel(q_ref, k_ref, v_ref, g_ref, cos_ref, sin_ref, lg_ref, o_ref, r_ref, *, tc):
    c = RET_CHUNK

    @pl.when(pl.program_id(2) == 0)
    def _():
        r_ref[...] = jnp.zeros_like(r_ref)

    lg = lg_ref[0:1, 0:1]
    rowi = lax.broadcasted_iota(jnp.int32, (c, c), 0)
    coli = lax.broadcasted_iota(jnp.int32, (c, c), 1)
    diff = (rowi - coli).astype(F32)
    decay = jnp.where(diff >= 0, jnp.exp(lg * jnp.maximum(diff, 0.0)), 0.0)
    pos = lax.broadcasted_iota(jnp.int32, (c, 1), 0).astype(F32)
    q_scale = jnp.exp(lg * (pos + 1.0))
    k_scale = jnp.exp(lg * (c - 1.0 - pos))
    chunk_decay = jnp.exp(lg * c)

    def rope(t, cos2, sin2):
        return t * cos2 + pltpu.roll(t, HEAD_DIM // 2, 1) * sin2

    for ci in range(tc // c):
        sl = slice(ci * c, (ci + 1) * c)
        cos2 = cos_ref[sl, :]
        sin2 = sin_ref[sl, :]
        q = rope(q_ref[sl, :].astype(F32), cos2, sin2)
        k = rope(k_ref[sl, :].astype(F32), cos2, sin2) * (HEAD_DIM ** -0.5)
        v = v_ref[sl, :]
        inner = _dot_nt(q.astype(BF16), k.astype(BF16)) * decay
        r_prev = r_ref[...]
        o = _dot(inner.astype(BF16), v) + _dot((q * q_scale).astype(BF16), r_prev.astype(BF16))
        r_ref[...] = r_prev * chunk_decay + _dot_tn((k * k_scale).astype(BF16), v)
        o = _rms_rows(o) * _silu(g_ref[sl, :].astype(F32))
        o_ref[sl, :] = o.astype(o_ref.dtype)


def retention_mix(p, cos2, sin2, lg_tab, batch, seq, n_heads, q_blk, k_blk, v_blk, g_blk, tc=512):
    t = p.shape[0]
    tc = min(tc, seq)
    ns = seq // tc
    kern = functools.partial(_ret_kernel, tc=tc)
    return pl.pallas_call(
        kern,
        out_shape=jax.ShapeDtypeStruct((t, n_heads * HEAD_DIM), BF16),
        grid=(batch, n_heads, ns),
        in_specs=[pl.BlockSpec((tc, HEAD_DIM), lambda b, h, s: (b * ns + s, q_blk + h)),
                  pl.BlockSpec((tc, HEAD_DIM), lambda b, h, s: (b * ns + s, k_blk + h)),
                  pl.BlockSpec((tc, HEAD_DIM), lambda b, h, s: (b * ns + s, v_blk + h)),
                  pl.BlockSpec((tc, HEAD_DIM), lambda b, h, s: (b * ns + s, g_blk + h)),
                  pl.BlockSpec((tc, HEAD_DIM), lambda b, h, s: (s, 0)),
                  pl.BlockSpec((tc, HEAD_DIM), lambda b, h, s: (s, 0)),
                  pl.BlockSpec((None, 8, LANES), lambda b, h, s: (h, 0, 0))],
        out_specs=pl.BlockSpec((tc, HEAD_DIM), lambda b, h, s: (b * ns + s, h)),
        scratch_shapes=[pltpu.VMEM((HEAD_DIM, HEAD_DIM), F32)],
        compiler_params=_cparams(3), name="retention",
    )(p, p, p, p, cos2, sin2, lg_tab)


def _dn_kernel(uq_ref, uk_ref, uv_ref, z_ref, cwq_ref, cwk_ref, cwv_ref, sm_ref, smt_ref,
               crow_ref, ccol_ref, ng_ref, o_ref,
               state_ref, bq_ref, bk_ref, bv_ref, qs_ref, ks_ref, vs_ref, *, tr):
    c = DN_CHUNK
    g_heads = DN_GROUP
    gw = g_heads * HEAD_DIM

    @pl.when(pl.program_id(2) == 0)
    def _():
        state_ref[...] = jnp.zeros_like(state_ref)
        bq_ref[0:8, :] = jnp.zeros((8, gw), F32)
        bk_ref[0:8, :] = jnp.zeros((8, gw), F32)
        bv_ref[0:8, :] = jnp.zeros((8, gw), F32)

    def conv(u_ref, cw_ref, buf_ref, dst_ref):
        buf_ref[8:8 + tr, :] = u_ref[...].astype(F32)
        y = cw_ref[3:4, :] * buf_ref[8:8 + tr, :]
        for j in range(CONV_K - 1):
            y = y + cw_ref[j:j + 1, :] * buf_ref[5 + j:5 + j + tr, :]
        buf_ref[0:8, :] = buf_ref[tr:tr + 8, :]
        dst_ref[...] = _silu(y)

    conv(uq_ref, cwq_ref, bq_ref, qs_ref)
    conv(uk_ref, cwk_ref, bk_ref, ks_ref)
    conv(uv_ref, cwv_ref, bv_ref, vs_ref)

    sm = sm_ref[...]
    beta_all = _sigmoid(sm)
    g_all = crow_ref[1:2, :] * _softplus(sm + crow_ref[0:1, :])
    smt = smt_ref[...]
    g_rows = ccol_ref[1, :, 0:1] * _softplus(smt + ccol_ref[0, :, 0:1])

    rowi = lax.broadcasted_iota(jnp.int32, (c, c), 0)
    coli = lax.broadcasted_iota(jnp.int32, (c, c), 1)
    incl = rowi >= coli
    strict = rowi > coli
    tri_l = jnp.where(incl, 1.0, 0.0).astype(BF16)
    tri_u = jnp.where(rowi <= coli, 1.0, 0.0).astype(BF16)
    ng = ng_ref[...]
    n_chunks = tr // c
    items = [(ci, hh) for ci in range(n_chunks) for hh in range(g_heads)]

    gc_cols, gc_rowsl = [], []
    for ci in range(n_chunks):
        sl = slice(ci * c, (ci + 1) * c)
        gh, gm, gl = _split3(g_all[sl, :])
        gc_cols.append(_dot(tri_l, gh) + _dot(tri_l, gm) + _dot(tri_l, gl))
        rh, rm, rl = _split3(g_rows[:, sl])
        gc_rowsl.append(_dot(rh, tri_u) + _dot(rm, tri_u) + _dot(rl, tri_u))

    qn, kn, vv, beta, gcc, gamma, qk, xm = {}, {}, {}, {}, {}, {}, {}, {}
    for it in items:
        ci, hh = it
        sl = slice(ci * c, (ci + 1) * c)
        hs = slice(hh * HEAD_DIM, (hh + 1) * HEAD_DIM)
        q = qs_ref[sl, hs]
        k = ks_ref[sl, hs]
        vv[it] = vs_ref[sl, hs]
        q = q * lax.rsqrt(jnp.sum(q * q, axis=-1, keepdims=True) + EPS) * (HEAD_DIM ** -0.5)
        k = k * lax.rsqrt(jnp.sum(k * k, axis=-1, keepdims=True) + EPS)
        qn[it], kn[it] = q, k
        beta[it] = beta_all[sl, hh:hh + 1]
        gcc[it] = gc_cols[ci][:, g_heads + hh:g_heads + hh + 1]
        gc_r = gc_rowsl[ci][g_heads + hh:g_heads + hh + 1, :]
        gamma[it] = jnp.exp(jnp.where(incl, gcc[it] - gc_r, -jnp.inf))
        kb = k.astype(BF16)
        qkk = _dot_nt(jnp.concatenate([q.astype(BF16), kb], axis=0), kb)
        qk[it] = qkk[:c]
        xm[it] = -jnp.where(strict, beta[it] * qkk[c:] * gamma[it], 0.0)

    n_lvls = int(np.log2(c)) - 3
    same8 = (rowi // 8) == (coli // 8)
    nm = {}
    for it in items:
        x8 = jnp.where(same8, xm[it], 0.0)
        xb = x8.astype(BF16)
        p1 = _dot(xb, xb)
        r = _dot(jnp.concatenate([x8, p1], axis=0).astype(BF16), p1.astype(BF16))
        n3 = x8 + p1 + r[:c]
        p2 = r[c:]
        nm[it] = n3 + p2 + _dot(n3.astype(BF16), p2.astype(BF16))
    for lvl in range(n_lvls):
        inner = (rowi // (8 << lvl)) == (coli // (8 << lvl))
        outer = (rowi // (16 << lvl)) == (coli // (16 << lvl))
        emask = outer & jnp.logical_not(inner)
        for it in items:
            e = jnp.where(emask, -xm[it], 0.0)
            m1 = e + _dot(nm[it].astype(BF16), e.astype(BF16))
            ded = m1 + _dot(m1.astype(BF16), nm[it].astype(BF16))
            nm[it] = nm[it] - ded

    q_eff, o_0, d_mat, c_mat, g_last = {}, {}, {}, {}, {}
    for it in items:
        e_gc = jnp.exp(gcc[it])
        rhs = jnp.concatenate([kn[it] * (beta[it] * e_gc), vv[it] * beta[it]], axis=1)
        solb = (rhs + _dot(nm[it].astype(BF16), rhs.astype(BF16))).astype(BF16)
        attn = (qk[it] * gamma[it]).astype(BF16)
        aw = _dot(attn, solb)
        gc_last = gcc[it][c - 1:c, :]
        k_dec = (kn[it] * jnp.exp(gc_last - gcc[it])).astype(BF16)
        kw = _dot_tn(k_dec, solb)
        q_eff[it] = (qn[it] * e_gc - aw[:, :HEAD_DIM]).astype(BF16)
        o_0[it] = aw[:, HEAD_DIM:]
        d_mat[it] = kw[:, :HEAD_DIM].astype(BF16)
        c_mat[it] = kw[:, HEAD_DIM:]
        g_last[it] = jnp.exp(gc_last)

    states = [state_ref[hh] for hh in range(g_heads)]
    for it in items:
        ci, hh = it
        sl = slice(ci * c, (ci + 1) * c)
        hs = slice(hh * HEAD_DIM, (hh + 1) * HEAD_DIM)
        st = states[hh]
        stb = st.astype(BF16)
        o = _dot(q_eff[it], stb) + o_0[it]
        states[hh] = st * g_last[it] + (c_mat[it] - _dot(d_mat[it], stb))
        o = _rms_rows(o) * ng * _silu(z_ref[sl, hs].astype(F32))
        o_ref[sl, hs] = o.astype(o_ref.dtype)
    for hh in range(g_heads):
        state_ref[hh] = states[hh]


def deltanet_mix(p, conv_w8, small, small_t, crow, ccol, norm_g, batch, seq, n_heads,
                 q_blk, k_blk, v_blk, z_blk, tr=512):
    t = p.shape[0]
    tr = min(tr, seq)
    ns = seq // tr
    ngroups = n_heads // DN_GROUP
    gw = DN_GROUP * HEAD_DIM
    kern = functools.partial(_dn_kernel, tr=tr)
    row = lambda b, g, s: b * ns + s
    return pl.pallas_call(
        kern,
        out_shape=jax.ShapeDtypeStruct((t, n_heads * HEAD_DIM), BF16),
        grid=(batch, ngroups, ns),
        in_specs=[pl.BlockSpec((tr, gw), lambda b, g, s: (row(b, g, s), q_blk + g)),
                  pl.BlockSpec((tr, gw), lambda b, g, s: (row(b, g, s), k_blk + g)),
                  pl.BlockSpec((tr, gw), lambda b, g, s: (row(b, g, s), v_blk + g)),
                  pl.BlockSpec((tr, gw), lambda b, g, s: (row(b, g, s), z_blk + g)),
                  pl.BlockSpec((8, gw), lambda b, g, s: (0, q_blk + g)),
                  pl.BlockSpec((8, gw), lambda b, g, s: (0, k_blk + g)),
                  pl.BlockSpec((8, gw), lambda b, g, s: (0, v_blk + g)),
                  pl.BlockSpec((tr, LANES), lambda b, g, s: (row(b, g, s), g)),
                  pl.BlockSpec((None, None, 8, tr), lambda b, g, s: (b, g, 0, s)),
                  pl.BlockSpec((None, 8, LANES), lambda b, g, s: (g, 0, 0)),
                  pl.BlockSpec((None, 2, 8, LANES), lambda b, g, s: (g, 0, 0, 0)),
                  pl.BlockSpec((1, HEAD_DIM), lambda b, g, s: (0, 0))],
        out_specs=pl.BlockSpec((tr, gw), lambda b, g, s: (row(b, g, s), g)),
        scratch_shapes=[pltpu.VMEM((DN_GROUP, HEAD_DIM, HEAD_DIM), F32),
                        pltpu.VMEM((tr + 8, gw), F32), pltpu.VMEM((tr + 8, gw), F32),
                        pltpu.VMEM((tr + 8, gw), F32),
                        pltpu.VMEM((tr, gw), F32), pltpu.VMEM((tr, gw), F32), pltpu.VMEM((tr, gw), F32)],
        compiler_params=_cparams(3), name="deltanet",
    )(p, p, p, p, conv_w8, conv_w8, conv_w8, small, small_t, crow, ccol, norm_g)


def _norm_router_kernel(x_ref, g_ref, sc_ref, sh_ref, wr_ref, br_ref, h_ref, lg_ref):
    x = x_ref[...]
    h = _rms_rows(x) * g_ref[...] * (1.0 + sc_ref[0]) + sh_ref[0]
    half = h.shape[1] // 2
    h_ref[...] = _bf16_bits_lo(h[:, :half]) | _bf16_bits_hi(h[:, half:])
    w = wr_ref[...]
    w_hi = w.astype(BF16)
    w_lo = (w - w_hi.astype(F32)).astype(BF16)
    h_hi = h.astype(BF16)
    h_lo = (h - h_hi.astype(F32)).astype(BF16)
    lg_ref[...] = _dot(h_hi, w_hi) + _dot(h_lo, w_hi) + _dot(h_hi, w_lo) + br_ref[...]


def norm_router(x2d, g, sc, sh, w_router_pad, b_router_pad, seq, tm=256):
    t, d = x2d.shape
    tm = min(tm, seq)
    nb = seq // tm
    return pl.pallas_call(
        _norm_router_kernel,
        out_shape=(jax.ShapeDtypeStruct((t, d // 2), U32), jax.ShapeDtypeStruct((t, LANES), F32)),
        grid=(t // tm,),
        in_specs=[pl.BlockSpec((tm, d), lambda i: (i, 0)),
                  pl.BlockSpec((1, d), lambda i: (0, 0)),
                  pl.BlockSpec((1, 1, d), lambda i: (i // nb, 0, 0)),
                  pl.BlockSpec((1, 1, d), lambda i: (i // nb, 0, 0)),
                  pl.BlockSpec((d, LANES), lambda i: (0, 0)),
                  pl.BlockSpec((1, LANES), lambda i: (0, 0))],
        out_specs=(pl.BlockSpec((tm, d // 2), lambda i: (i, 0)),
                   pl.BlockSpec((tm, LANES), lambda i: (i, 0))),
        compiler_params=_cparams(1), name="norm_router",
    )(x2d, g.reshape(1, d), sc, sh, w_router_pad, b_router_pad)


def _moe_kernel(blk_e_ref, first_ref, nxt_ref, nused_ref, tok_ref, tokn_ref, h_hbm, wgu_hbm, wd_hbm,
                bgu_ref, bd_ref, y_ref,
                stg_gu, stg_d, wgu_b, wd_b, perm_ref, xbuf0, xbuf1, wsem, gsem, *, tb, layer):
    i = pl.program_id(0)
    nused = nused_ref[0]
    d, f2 = stg_gu.shape
    f = f2 // 2
    xbufs = (xbuf0, xbuf1)

    def weight_copies(e):
        return (pltpu.make_async_copy(wgu_hbm.at[layer, e], stg_gu, wsem.at[0]),
                pltpu.make_async_copy(wd_hbm.at[layer, e], stg_d, wsem.at[1]))

    def row_copy(toks, r, s):
        return pltpu.make_async_copy(h_hbm.at[toks[0, r]],
                                     xbufs[s].at[pl.ds(r, 1), :], gsem.at[s])

    def wait_gather(s):
        pltpu.make_async_copy(xbufs[s], xbufs[s], gsem.at[s]).wait()

    @pl.when(i == 0)
    def _():
        for cp in weight_copies(blk_e_ref[0]):
            cp.start(priority=1)

        def issue(r, carry):
            row_copy(tok_ref, r, 0).start()
            return carry
        lax.fori_loop(0, tb, issue, 0)
        rr = lax.broadcasted_iota(jnp.int32, (f2, f), 0)
        cc = lax.broadcasted_iota(jnp.int32, (f2, f), 1)
        perm_ref[...] = jnp.where(rr == 2 * cc, 1.0, 0.0).astype(BF16)

    @pl.when((i < nused) & (first_ref[i] == 1))
    def _():
        for cp in weight_copies(0):
            cp.wait()
        rows = min(256, f)

        def cast_gu(r, carry):
            r0 = pl.multiple_of(r * rows, rows)
            wgu_b[pl.ds(r0, rows), :] = stg_gu[pl.ds(r0, rows), :].astype(BF16)
            return carry
        lax.fori_loop(0, d // rows, cast_gu, 0)

        def cast_d(r, carry):
            r0 = pl.multiple_of(r * rows, rows)
            wd_b[pl.ds(r0, rows), :] = stg_d[pl.ds(r0, rows), :].astype(BF16)
            return carry
        lax.fori_loop(0, f // rows, cast_d, 0)

        @pl.when(nxt_ref[i] >= 0)
        def _():
            for cp in weight_copies(nxt_ref[i]):
                cp.start(priority=1)

    def compute(s):
        wait_gather(s)
        xw = xbufs[s][...]
        x = jnp.concatenate([pltpu.bitcast(xw << 16, F32).astype(BF16),
                             pltpu.bitcast(xw & jnp.uint32(HI16), F32).astype(BF16)], axis=1)
        for r in range(tb):
            row_copy(tokn_ref, r, 1 - s).start()
        gu = _dot(x, wgu_b[...]) + bgu_ref[...]
        gate = jnp.minimum(gu, SWIGLU_LIMIT)
        sg = gate * _sigmoid(SWIGLU_ALPHA * gate)
        lin1 = jnp.clip(gu, -SWIGLU_LIMIT, SWIGLU_LIMIT) + 1.0
        act_il = sg * pltpu.roll(lin1, f2 - 1, 1)
        act = _dot(act_il.astype(BF16), perm_ref[...])
        y = _dot(act.astype(BF16), wd_b[...]) + bd_ref[...]
        y_ref[...] = _bf16_bits_lo(y[:, :d // 2]) | _bf16_bits_hi(y[:, d // 2:])

    for s in range(2):
        @pl.when((i < nused) & (i % 2 == s))
        def _(s=s):
            compute(s)

        @pl.when((i == nused) & (i % 2 == s))
        def _(s=s):
            wait_gather(s)

    @pl.when(i >= nused)
    def _():
        y_ref[...] = jnp.zeros_like(y_ref)


def moe_experts(h2, w_gu, w_down, b_gu_l, b_down_l, layer, blk_e, first, nxt, n_used, row_tok, tb):
    d = w_gu.shape[2]
    n_blk = row_tok.shape[0]
    f2 = w_gu.shape[3]
    f = f2 // 2
    kern = functools.partial(_moe_kernel, tb=tb, layer=layer)
    grid_spec = pltpu.PrefetchScalarGridSpec(
        num_scalar_prefetch=4,
        grid=(n_blk,),
        in_specs=[pl.BlockSpec((None, 1, tb), lambda i, *_: (i, 0, 0), memory_space=pltpu.SMEM),
                  pl.BlockSpec((None, 1, tb), lambda i, *_: (jnp.minimum(i + 1, n_blk - 1), 0, 0),
                               memory_space=pltpu.SMEM),
                  pl.BlockSpec(memory_space=pl.ANY),
                  pl.BlockSpec(memory_space=pl.ANY),
                  pl.BlockSpec(memory_space=pl.ANY),
                  pl.BlockSpec((None, 1, f2), lambda i, be, *_: (be[i], 0, 0)),
                  pl.BlockSpec((None, 1, d), lambda i, be, *_: (be[i], 0, 0))],
        out_specs=pl.BlockSpec((tb, d // 2), lambda i, *_: (i, 0)),
        scratch_shapes=[pltpu.VMEM((d, f2), F32), pltpu.VMEM((f, d), F32),
                        pltpu.VMEM((d, f2), BF16), pltpu.VMEM((f, d), BF16),
                        pltpu.VMEM((f2, f), BF16), pltpu.VMEM((tb, d // 2), U32),
                        pltpu.VMEM((tb, d // 2), U32),
                        pltpu.SemaphoreType.DMA((2,)), pltpu.SemaphoreType.DMA((2,))])
    return pl.pallas_call(
        kern,
        out_shape=jax.ShapeDtypeStruct((n_blk * tb, d // 2), U32),
        grid_spec=grid_spec,
        compiler_params=_cparams(1, MOE_VMEM_LIMIT), name="moe_experts",
    )(blk_e, first, nxt, n_used, row_tok, row_tok, h2, w_gu, w_down, b_gu_l, b_down_l)


def _combine_kernel(pos_ref, posn_ref, y_hbm, x_ref, tw_ref, g2_ref, ng_ref, sc_ref, sh_ref, *rest,
                    tm, final, n_steps):
    if final:
        out_ref, ybuf0, ybuf1, sem = rest
    else:
        xo_ref, h_ref, ybuf0, ybuf1, sem = rest
    i = pl.program_id(0)
    ybufs = (ybuf0, ybuf1)

    def start_gather(p_ref, s):
        def issue(r, carry):
            for k in range(TOP_K):
                pltpu.make_async_copy(y_hbm.at[pl.ds(p_ref[0, r * TOP_K + k], 1), :],
                                      ybufs[s].at[k, pl.ds(r, 1), :], sem.at[s]).start()
            return carry
        lax.fori_loop(0, tm, issue, 0)

    def wait_gather(s):
        for k in range(TOP_K):
            pltpu.make_async_copy(y_hbm.at[pl.ds(0, tm), :], ybufs[s].at[k], sem.at[s]).wait()

    def lo(w):
        return pltpu.bitcast(w << 16, F32)

    def hi(w):
        return pltpu.bitcast(w & jnp.uint32(HI16), F32)

    def finish(s):
        tw = tw_ref[...]
        acc_lo = acc_hi = None
        for k in range(TOP_K):
            w = ybufs[s][k]
            wk = tw[:, k:k + 1]
            acc_lo = wk * lo(w) if acc_lo is None else acc_lo + wk * lo(w)
            acc_hi = wk * hi(w) if acc_hi is None else acc_hi + wk * hi(w)
        x = x_ref[...] + g2_ref[0] * jnp.concatenate([acc_lo, acc_hi], axis=1)
        if final:
            out_ref[...] = _rms_rows(x) * ng_ref[...]
        else:
            xo_ref[...] = x
            h = _rms_rows(x) * ng_ref[...] * (1.0 + sc_ref[0]) + sh_ref[0]
            h_ref[...] = h.astype(h_ref.dtype)

    @pl.when(i == 0)
    def _():
        start_gather(pos_ref, 0)

    for s in range(2):
        @pl.when(i % 2 == s)
        def _(s=s):
            @pl.when(i + 1 < n_steps)
            def _():
                start_gather(posn_ref, 1 - s)
            wait_gather(s)
            finish(s)


def moe_combine(y, pos, top_w_pad, x2d, gate2, norm_g, sc, sh, seq, final, tm=128):
    t, d = x2d.shape
    tm = min(tm, seq)
    nb = seq // tm
    n_steps = t // tm
    kern = functools.partial(_combine_kernel, tm=tm, final=final, n_steps=n_steps)
    if final:
        out_shape = jax.ShapeDtypeStruct((t, d), F32)
        out_specs = pl.BlockSpec((tm, d), lambda i: (i, 0))
    else:
        out_shape = (jax.ShapeDtypeStruct((t, d), F32), jax.ShapeDtypeStruct((t, d), BF16))
        out_specs = (pl.BlockSpec((tm, d), lambda i: (i, 0)), pl.BlockSpec((tm, d), lambda i: (i, 0)))
    pos3 = pos.reshape(n_steps, 1, tm * TOP_K)
    return pl.pallas_call(
        kern,
        out_shape=out_shape,
        grid=(n_steps,),
        in_specs=[pl.BlockSpec((None, 1, tm * TOP_K), lambda i: (i, 0, 0), memory_space=pltpu.SMEM),
                  pl.BlockSpec((None, 1, tm * TOP_K), lambda i: (jnp.minimum(i + 1, n_steps - 1), 0, 0),
                               memory_space=pltpu.SMEM),
                  pl.BlockSpec(memory_space=pl.ANY),
                  pl.BlockSpec((tm, d), lambda i: (i, 0)),
                  pl.BlockSpec((tm, LANES), lambda i: (i, 0)),
                  pl.BlockSpec((1, 1, d), lambda i: (i // nb, 0, 0)),
                  pl.BlockSpec((1, d), lambda i: (0, 0)),
                  pl.BlockSpec((1, 1, d), lambda i: (i // nb, 0, 0)),
                  pl.BlockSpec((1, 1, d), lambda i: (i // nb, 0, 0))],
        out_specs=out_specs,
        scratch_shapes=[pltpu.VMEM((TOP_K, tm, d // 2), U32), pltpu.VMEM((TOP_K, tm, d // 2), U32),
                        pltpu.SemaphoreType.DMA((2,))],
        compiler_params=_cparams(1), name="moe_combine",
    )(pos3, pos3, y, x2d, top_w_pad, gate2, norm_g.reshape(1, d), sc, sh)


def _routing_tables(logits, n_experts, tb):
    t = logits.shape[0]
    top_v, top_e = lax.top_k(logits[:, :n_experts], TOP_K)
    top_w = jax.nn.softmax(top_v, axis=-1)
    n_a = t * TOP_K
    eids = jnp.arange(n_experts, dtype=jnp.int32)
    sel = jnp.any(top_e[:, :, None] == eids[None, None, :], axis=1).astype(jnp.int32)
    csum = jnp.cumsum(sel, axis=0)
    counts = csum[-1]
    rank = jnp.take_along_axis(csum - sel, top_e, axis=1)
    padded = (counts + tb - 1) // tb * tb
    pend = jnp.cumsum(padded)
    pstart = pend - padded
    pos = (pstart[top_e] + rank).astype(jnp.int32).reshape(n_a)
    n_rows = (n_a + n_experts * (tb - 1) + tb - 1) // tb * tb + tb
    n_blk = n_rows // tb
    t_a = jnp.repeat(jnp.arange(t, dtype=jnp.int32), TOP_K)
    row_tok = jnp.zeros((n_rows,), jnp.int32).at[pos].set(t_a)
    blk_start = jnp.arange(n_blk, dtype=jnp.int32) * tb
    blk_e = jnp.minimum(jnp.sum((pend[None, :] <= blk_start[:, None]).astype(jnp.int32), axis=1),
                        n_experts - 1).astype(jnp.int32)
    n_used = (pend[-1] // tb).astype(jnp.int32)
    used = jnp.arange(n_blk) < n_used
    first = (used & (blk_start == pstart[blk_e])).astype(jnp.int32)
    has = counts > 0
    cand = jnp.where(has, eids, n_experts)
    suffix_min = lax.cummin(cand, axis=0, reverse=True)
    nxt_e = jnp.concatenate([suffix_min[1:], jnp.full((1,), n_experts, jnp.int32)])
    nxt_e = jnp.where(nxt_e >= n_experts, -1, nxt_e).astype(jnp.int32)
    nxt = nxt_e[blk_e]
    top_w_pad = jnp.pad(top_w, ((0, 0), (0, LANES - TOP_K)))
    return pos, top_w_pad, row_tok.reshape(n_blk, 1, tb), blk_e, first, nxt, n_used.reshape(1)


def _rope_tables(seq):
    half = HEAD_DIM // 2
    inv = ROPE_BASE ** (-jnp.arange(half, dtype=F32) / half)
    ang = jnp.arange(seq, dtype=F32)[:, None] * inv[None, :]
    cos, sin = jnp.cos(ang), jnp.sin(ang)
    return jnp.concatenate([cos, cos], axis=-1), jnp.concatenate([-sin, sin], axis=-1)


def kernel(x, c, norm1_g, w_ada, b_ada, w_in, dn_conv_w, dn_a_log, dn_dt_bias, dn_norm_g, fox_f_bias,
           w_out, norm2_g, w_router, b_router, w_gu, b_gu, w_down, b_down, final_norm_g):
    batch, seq, d = x.shape
    depth = w_ada.shape[0]
    t = batch * seq
    dn_heads = dn_a_log.shape[1]
    fox_heads = fox_f_bias.shape[1]
    ret_heads = d // HEAD_DIM - dn_heads - fox_heads
    dn_w, ret_w, fox_w = dn_heads * HEAD_DIM, ret_heads * HEAD_DIM, fox_heads * HEAD_DIM
    n_experts = w_router.shape[2]
    n_groups = dn_heads // DN_GROUP
    gw = DN_GROUP * HEAD_DIM
    moe_tb = 256
    tn = 512

    widths = (3 * dn_w, dn_w, dn_heads, dn_heads, ret_w, ret_w, ret_w, ret_w,
              fox_w, fox_w, fox_w, fox_w, fox_heads)
    cuts = np.concatenate([[0], np.cumsum(widths)])
    o_b, o_a, o_r, o_ff = int(cuts[2]), int(cuts[3]), int(cuts[4]), int(cuts[12])
    n_main = o_b + (o_ff - o_r)
    small_src, small_dst = [], []
    for g in range(n_groups):
        for hh in range(DN_GROUP):
            small_src += [g * DN_GROUP + hh, dn_heads + g * DN_GROUP + hh]
            small_dst += [g * LANES + hh, g * LANES + DN_GROUP + hh]
    for hh in range(fox_heads):
        small_src.append(2 * dn_heads + hh)
        small_dst.append(n_groups * LANES + hh)
    small_src = np.asarray(small_src, np.int32)
    small_dst = np.asarray(small_dst, np.int32)
    w_in_t = jnp.transpose(w_in, (2, 0, 1))
    w_main_t = w_in_prep(w_in_t, o_b, o_r - o_b, n_main)
    w_sc = jnp.concatenate([w_in_t[o_b:o_r], w_in_t[o_ff:]], axis=0)
    w_small_t = jnp.zeros((depth, (n_groups + 1) * LANES, d), BF16).at[:, small_dst].set(
        jnp.transpose(w_sc, (1, 0, 2))[:, small_src].astype(BF16))

    x2d = x.reshape(t, d)
    c_pad = jnp.zeros((8, d), BF16).at[:batch].set(c.astype(BF16))
    mod = ada_mod(c_pad, w_ada, b_ada)[:, :batch, :]
    mod = mod.reshape(depth, batch, 6, 1, d)
    cos2, sin2 = _rope_tables(seq)
    lg_tab = jnp.log(1.0 - 2.0 ** (-5.0 - jnp.arange(ret_heads, dtype=F32)))
    lg_tab = jnp.broadcast_to(lg_tab[:, None, None], (ret_heads, 8, LANES))

    dn_q_blk, dn_k_blk, dn_v_blk, dn_z_blk = 0, dn_w // gw, 2 * dn_w // gw, 3 * dn_w // gw
    r0 = o_b // HEAD_DIM
    r_q_blk, r_k_blk, r_v_blk, r_g_blk = r0, r0 + ret_heads, r0 + 2 * ret_heads, r0 + 3 * ret_heads
    f0 = r0 + 4 * ret_heads
    f_q_blk, f_k_blk, f_v_blk, f_z_blk = f0, f0 + fox_heads, f0 + 2 * fox_heads, f0 + 3 * fox_heads

    h = norm_mod(x2d, norm1_g[0], mod[0, :, 1], mod[0, :, 0], seq)
    out = None
    for l in range(depth):
        sh1, sc1, g1, sh2, sc2, g2 = (mod[l, :, i] for i in range(6))
        p = matmul_nt(h, w_main_t, l, BF16, "in_proj")
        small = matmul_nt(h, w_small_t, l, F32, "small_proj")

        conv_w8 = jnp.zeros((8, 3 * dn_w), F32).at[:CONV_K].set(dn_conv_w[l])
        small_t = small[:, :n_groups * LANES].reshape(batch, seq, n_groups, LANES)[..., :8]
        small_t = small_t.transpose(0, 2, 3, 1)
        dtb = dn_dt_bias[l].reshape(n_groups, DN_GROUP)
        nega = -jnp.exp(dn_a_log[l]).reshape(n_groups, DN_GROUP)
        zg = jnp.zeros((n_groups, DN_GROUP), F32)
        pad = jnp.zeros((n_groups, LANES - 2 * DN_GROUP), F32)
        crow = jnp.stack([jnp.concatenate([zg, dtb, pad], axis=1),
                          jnp.concatenate([zg, nega, pad], axis=1)], axis=1)
        crow = jnp.concatenate([crow, jnp.zeros((n_groups, 6, LANES), F32)], axis=1)
        ccol = jnp.stack([jnp.concatenate([zg, dtb], axis=1), jnp.concatenate([zg, nega], axis=1)], axis=1)
        ccol = jnp.broadcast_to(ccol[..., None], (n_groups, 2, 8, LANES))
        o_dn = deltanet_mix(p, conv_w8, small, small_t, crow, ccol, dn_norm_g[l].reshape(1, HEAD_DIM),
                            batch, seq, dn_heads, dn_q_blk, dn_k_blk, dn_v_blk, dn_z_blk)
        o_ret = retention_mix(p, cos2, sin2, lg_tab, batch, seq, ret_heads,
                              r_q_blk, r_k_blk, r_v_blk, r_g_blk)
        fb = jnp.zeros((1, LANES), F32).at[0, :fox_heads].set(fox_f_bias[l])
        f_cum = fox_prefix(small, fb, batch, seq, n_groups)
        o_fox = fox_attention(p, f_cum, batch, seq, fox_heads, f_q_blk, f_k_blk, f_v_blk, f_z_blk)
        x2d = out_proj_resid(o_dn, o_ret, o_fox, w_out, l, x2d, g1, seq)

        wr_pad = jnp.zeros((d, LANES), F32).at[:, :n_experts].set(w_router[l])
        br_pad = jnp.zeros((1, LANES), F32).at[0, :n_experts].set(b_router[l])
        h2, logits = norm_router(x2d, norm2_g[l], sc2, sh2, wr_pad, br_pad, seq)
        pos, top_w_pad, row_tok, blk_e, first, nxt, n_used = _routing_tables(logits, n_experts, moe_tb)
        y = moe_experts(h2.reshape(t, 1, d // 2), w_gu, w_down, b_gu[l][:, None, :], b_down[l][:, None, :], l,
                        blk_e, first, nxt, n_used, row_tok, moe_tb)
        if l + 1 < depth:
            nsh1, nsc1 = mod[l + 1, :, 0], mod[l + 1, :, 1]
            x2d, h = moe_combine(y, pos, top_w_pad, x2d, g2, norm1_g[l + 1], nsc1, nsh1, seq, final=False)
        else:
            out = moe_combine(y, pos, top_w_pad, x2d, g2, final_norm_g, sc2, sh2, seq, final=True)
    return out.reshape(batch, seq, d)
```
